```python
import jax, jax.numpy as jnp
from jax import lax
import numpy as np

D_MODEL = 4096
BATCH = 2
SEQ = 4096
DEPTH = 2

CTX_LEN = 256
GRID_W = 64
N_MOD = 6
W_A = 2048
H_A = 16
HD_A = W_A // H_A
CONV_A = 4
CONV_A_PAD = (1, 2)
LRU_C = 8.0
W_B = 2048
G_B = 16
CHUNK = 128
W_C = 2048
CONV_C = 31
CONV_C_PAD = (CONV_C // 2, CONV_C // 2)
N_BRANCH = 3
OFF_GA = 0
OFF_XA = OFF_GA + W_A
OFF_UV = OFF_XA + W_A
OFF_GLU = OFF_UV + 2 * W_B
OFF_GATE = OFF_GLU + 2 * W_C
N_IN = OFF_GATE + N_BRANCH * D_MODEL
N_EXP = 16
D_EXP = 1024
CAP_FACTOR = 2
EPS = 1e-6

kernel_name = "hybrid_rglru_sgmlp_conformer_ecmoe_dit"


def rmsnorm(x, g):
    xf = x.astype(jnp.float32)
    y = xf * lax.rsqrt(jnp.mean(xf * xf, axis=-1, keepdims=True) + EPS)
    return (y * g.astype(jnp.float32)).astype(x.dtype)


def layernorm(x, g, b):
    xf = x.astype(jnp.float32)
    mu = jnp.mean(xf, axis=-1, keepdims=True)
    var = jnp.mean(jnp.square(xf - mu), axis=-1, keepdims=True)
    y = (xf - mu) * lax.rsqrt(var + EPS)
    return (y * g.astype(jnp.float32) + b.astype(jnp.float32)).astype(x.dtype)


def modulate(h, shift, scale):
    return h * (1 + scale[:, None, :]) + shift[:, None, :]


def dwconv1d(x, w, b, pad):
    C = x.shape[-1]
    y = lax.conv_general_dilated(x, w[:, None, :].astype(x.dtype), window_strides=(1,), padding=[pad],
                                 dimension_numbers=('NWC', 'WIO', 'NWC'), feature_group_count=C)
    return y + b


def linear_scan(a, b, h0):
    b = b.at[:, 0].add(a[:, 0] * h0)

    def combine(l, r):
        return (l[0] * r[0], r[0] * l[1] + r[1])

    _, h = lax.associative_scan(combine, (a, b), axis=1)
    return h


def rglru_coeffs(xs, w_r, b_r, w_i, b_i, lam):
    B_, T, _ = xs.shape
    xf = xs.astype(jnp.float32)
    xh = xf.reshape(B_, T, H_A, HD_A)
    r = jax.nn.sigmoid(jnp.einsum('bthi,hij->bthj', xh, w_r.astype(jnp.float32)).reshape(B_, T, W_A)
                       + b_r.astype(jnp.float32))
    i = jax.nn.sigmoid(jnp.einsum('bthi,hij->bthj', xh, w_i.astype(jnp.float32)).reshape(B_, T, W_A)
                       + b_i.astype(jnp.float32))
    log_a = -LRU_C * r * jax.nn.softplus(-lam.astype(jnp.float32))
    a = jnp.exp(log_a)
    mult = jnp.sqrt(-jnp.expm1(2.0 * log_a))
    return a, mult * (i * xf)


def rglru_bidir(xc, xl, w_r, b_r, w_i, b_i, lam, need_ctx):
    B_ = xc.shape[0]
    zero = jnp.zeros((B_, W_A), jnp.float32)
    ac, bc = rglru_coeffs(xc, w_r[0], b_r[0], w_i[0], b_i[0], lam[0])
    hcf = linear_scan(ac, bc, zero)
    al, bl = rglru_coeffs(xl, w_r[0], b_r[0], w_i[0], b_i[0], lam[0])
    hlf = linear_scan(al, bl, hcf[:, -1])
    ac, bc = rglru_coeffs(jnp.flip(xc, 1), w_r[1], b_r[1], w_i[1], b_i[1], lam[1])
    hcb = linear_scan(ac, bc, zero)
    al, bl = rglru_coeffs(jnp.flip(xl, 1), w_r[1], b_r[1], w_i[1], b_i[1], lam[1])
    hlb = jnp.flip(linear_scan(al, bl, hcb[:, -1]), 1)
    yl = (hlf + hlb).astype(xl.dtype)
    yc = (hcf + jnp.flip(hcb, 1)).astype(xc.dtype) if need_ctx else None
    return yc, yl


def spatial_gating(z, ln_g, ln_b, w_s, b_s):
    z = jax.nn.gelu(z)
    u, v = jnp.split(z, 2, axis=-1)
    v = layernorm(v, ln_g, ln_b)
    B_, T, _ = v.shape
    vh = v.reshape(B_, T // CHUNK, CHUNK, G_B, W_B // G_B)
    s = jnp.einsum('gts,bnsgc->bntgc', w_s, vh) + b_s.T[None, None, :, :, None]
    return u * s.reshape(B_, T, W_B)


def conformer_conv(z, w, b, ln_g, ln_b, rows):
    p, q = jnp.split(z, 2, axis=-1)
    g = p * jax.nn.sigmoid(q)
    B_, T, C = g.shape
    if rows is None:
        y = dwconv1d(g, w, b, CONV_C_PAD)
    else:
        g4 = g.reshape(B_, rows, GRID_W, C)
        y = lax.conv_general_dilated(g4, w[:, None, None, :].astype(g.dtype), window_strides=(1, 1),
                                     padding=[CONV_C_PAD, (0, 0)],
                                     dimension_numbers=('NHWC', 'HWIO', 'NHWC'),
                                     feature_group_count=C).reshape(B_, T, C) + b
    return jax.nn.silu(layernorm(y, ln_g, ln_b))


def merge_branches(zg, ya, yb, yc, w_pa, w_pb, w_pc, w_out):
    ga, gb, gc = jnp.split(jax.nn.sigmoid(zg), N_BRANCH, axis=-1)
    y = ga * (ya @ w_pa) + gb * (yb @ w_pb) + gc * (yc @ w_pc)
    return y @ w_out


def token_mixer(hc, hl, rows, need_ctx, w_in, b_in, rg_conv_w, rg_conv_b, rg_w_r, rg_b_r, rg_w_i, rg_b_i,
                rg_lambda, sg_ln_g, sg_ln_b, sg_w, sg_b, cv_w, cv_b, cv_ln_g, cv_ln_b,
                w_proj_a, w_proj_b, w_proj_c, w_out):
    sl_xa = slice(OFF_XA, OFF_XA + W_A)
    zl = hl @ w_in + b_in
    if need_ctx:
        zc = hc @ w_in + b_in
        xa_c = zc[..., sl_xa]
    else:
        xa_c = hc @ w_in[:, sl_xa] + b_in[sl_xa]
    xa_c = dwconv1d(xa_c, rg_conv_w, rg_conv_b, CONV_A_PAD)
    xa_l = dwconv1d(zl[..., sl_xa], rg_conv_w, rg_conv_b, CONV_A_PAD)
    hc_a, hl_a = rglru_bidir(xa_c, xa_l, rg_w_r, rg_b_r, rg_w_i, rg_b_i, rg_lambda, need_ctx)
    ya_l = hl_a * jax.nn.gelu(zl[..., OFF_GA:OFF_XA])
    yb_l = spatial_gating(zl[..., OFF_UV:OFF_GLU], sg_ln_g, sg_ln_b, sg_w, sg_b)
    yc_l = conformer_conv(zl[..., OFF_GLU:OFF_GATE], cv_w, cv_b, cv_ln_g, cv_ln_b, rows)
    out_l = merge_branches(zl[..., OFF_GATE:], ya_l, yb_l, yc_l, w_proj_a, w_proj_b, w_proj_c, w_out)
    if not need_ctx:
        return None, out_l
    ya_c = hc_a * jax.nn.gelu(zc[..., OFF_GA:OFF_XA])
    yb_c = spatial_gating(zc[..., OFF_UV:OFF_GLU], sg_ln_g, sg_ln_b, sg_w, sg_b)
    yc_c = conformer_conv(zc[..., OFF_GLU:OFF_GATE], cv_w, cv_b, cv_ln_g, cv_ln_b, None)
    out_c = merge_branches(zc[..., OFF_GATE:], ya_c, yb_c, yc_c, w_proj_a, w_proj_b, w_proj_c, w_out)
    return out_c, out_l


def expert_choice_ffn(h, w_router, w_g, w_u, w_d):
    B_, n, _ = h.shape
    cap = CAP_FACTOR * n // N_EXP
    aff = jax.nn.softmax((h @ w_router).astype(jnp.float32), axis=-1)
    gate, idx = lax.top_k(jnp.swapaxes(aff, 1, 2), cap)
    xg = jax.vmap(lambda hb, ib: hb[ib])(h, idx)
    hid = jax.nn.silu(jnp.einsum('becd,edf->becf', xg, w_g)) * jnp.einsum('becd,edf->becf', xg, w_u)
    ye = jnp.einsum('becf,efd->becd', hid, w_d) * gate[..., None].astype(h.dtype)
    bidx = jnp.arange(B_)[:, None, None]
    return jnp.zeros_like(h).at[bidx, idx].add(ye)


def setup_inputs(seed: int = 0) -> dict:
    key = jax.random.key(seed)
    ks = iter(jax.random.split(key, 48))
    D = D_MODEL

    def nrm(shape, scale):
        return jax.random.normal(next(ks), shape, jnp.float32) * scale

    def gain(shape):
        return 1.0 + nrm(shape, 0.02)

    lam_u = jax.random.uniform(next(ks), (DEPTH, 2, W_A), jnp.float32, minval=0.9, maxval=0.999)
    rg_lambda = jnp.log(lam_u) - jnp.log1p(-lam_u)
    return {
        "x": nrm((BATCH, SEQ, D), 1.0),
        "c": nrm((BATCH, D), 1.0),
        "ctx": nrm((BATCH, CTX_LEN, D), 1.0),
        "c_ctx": nrm((D,), 1.0),
        "w_mod": nrm((DEPTH, D, N_MOD * D), 0.5 * D ** -0.5),
        "b_mod": nrm((DEPTH, N_MOD * D), 0.02),
        "g_mix": gain((DEPTH, D)),
        "g_ffn": gain((DEPTH, D)),
        "w_in": nrm((DEPTH, D, N_IN), D ** -0.5),
        "b_in": nrm((DEPTH, N_IN), 0.02),
        "rg_conv_w": nrm((DEPTH, CONV_A, W_A), CONV_A ** -0.5),
        "rg_conv_b": nrm((DEPTH, W_A), 0.02),
        "rg_w_r": nrm((DEPTH, 2, H_A, HD_A, HD_A), HD_A ** -0.5),
        "rg_b_r": nrm((DEPTH, 2, W_A), 0.02),
        "rg_w_i": nrm((DEPTH, 2, H_A, HD_A, HD_A), HD_A ** -0.5),
        "rg_b_i": nrm((DEPTH, 2, W_A), 0.02),
        "rg_lambda": rg_lambda,
        "sg_ln_g": gain((DEPTH, W_B)),
        "sg_ln_b": nrm((DEPTH, W_B), 0.02),
        "sg_w": nrm((DEPTH, G_B, CHUNK, CHUNK), 0.5 * CHUNK ** -0.5),
        "sg_b": gain((DEPTH, G_B, CHUNK)),
        "cv_w": nrm((DEPTH, CONV_C, W_C), CONV_C ** -0.5),
        "cv_b": nrm((DEPTH, W_C), 0.02),
        "cv_ln_g": gain((DEPTH, W_C)),
        "cv_ln_b": nrm((DEPTH, W_C), 0.02),
        "w_proj_a": nrm((DEPTH, W_A, D), W_A ** -0.5),
        "w_proj_b": nrm((DEPTH, W_B, D), W_B ** -0.5),
        "w_proj_c": nrm((DEPTH, W_C, D), W_C ** -0.5),
        "w_out": nrm((DEPTH, D, D), D ** -0.5),
        "w_router": nrm((DEPTH, D, N_EXP), D ** -0.5),
        "w_e_gate": nrm((DEPTH, N_EXP, D, D_EXP), D ** -0.5),
        "w_e_up": nrm((DEPTH, N_EXP, D, D_EXP), D ** -0.5),
        "w_e_down": nrm((DEPTH, N_EXP, D_EXP, D), D_EXP ** -0.5),
        "g_final": gain((D,)),
    }


def reference(x, c, ctx, c_ctx, w_mod, b_mod, g_mix, g_ffn, w_in, b_in, rg_conv_w, rg_conv_b,
              rg_w_r, rg_b_r, rg_w_i, rg_b_i, rg_lambda, sg_ln_g, sg_ln_b, sg_w, sg_b,
              cv_w, cv_b, cv_ln_g, cv_ln_b, w_proj_a, w_proj_b, w_proj_c, w_out,
              w_router, w_e_gate, w_e_up, w_e_down, g_final):
    rows = x.shape[1] // GRID_W
    for l in range(DEPTH):
        need_ctx = l < DEPTH - 1
        mod_l = jnp.split(jax.nn.silu(c) @ w_mod[l] + b_mod[l], N_MOD, axis=-1)
        mod_c = jnp.split((jax.nn.silu(c_ctx) @ w_mod[l] + b_mod[l])[None], N_MOD, axis=-1)
        hl = modulate(rmsnorm(x, g_mix[l]), mod_l[0], mod_l[1])
        hc = modulate(rmsnorm(ctx, g_mix[l]), mod_c[0], mod_c[1])
        oc, ol = token_mixer(hc, hl, rows, need_ctx, w_in[l], b_in[l], rg_conv_w[l], rg_conv_b[l],
                             rg_w_r[l], rg_b_r[l], rg_w_i[l], rg_b_i[l], rg_lambda[l],
                             sg_ln_g[l], sg_ln_b[l], sg_w[l], sg_b[l], cv_w[l], cv_b[l], cv_ln_g[l], cv_ln_b[l],
                             w_proj_a[l], w_proj_b[l], w_proj_c[l], w_out[l])
        x = x + mod_l[2][:, None, :] * ol
        hl = modulate(rmsnorm(x, g_ffn[l]), mod_l[3], mod_l[4])
        x = x + mod_l[5][:, None, :] * expert_choice_ffn(hl, w_router[l], w_e_gate[l], w_e_up[l], w_e_down[l])
        if need_ctx:
            ctx = ctx + mod_c[2][:, None, :] * oc
            hc = modulate(rmsnorm(ctx, g_ffn[l]), mod_c[3], mod_c[4])
            ctx = ctx + mod_c[5][:, None, :] * expert_choice_ffn(hc, w_router[l], w_e_gate[l], w_e_up[l],
                                                                 w_e_down[l])
    return rmsnorm(x, g_final)
```

```python
import functools

import jax
import jax.numpy as jnp
from jax import lax
from jax.experimental import pallas as pl
from jax.experimental.pallas import tpu as pltpu

GRID_W = 64
N_MOD = 6
H_A = 16
CONV_A = 4
LRU_C = 8.0
G_B = 16
CHUNK = 128
CONV_C = 31
N_BRANCH = 3
N_EXP = 16
CAP_FACTOR = 2
EPS = 1e-6

LANES = 128
SUB = 8
SEG = 256
VMEM_LIMIT = 56 * 1024 * 1024
MM_TN = 256
MM_TM_MAX = 2304
ROW_CHUNK = 32

BF16 = jnp.bfloat16
F32 = jnp.float32


def _cp(*sem):
    return pltpu.CompilerParams(dimension_semantics=sem, vmem_limit_bytes=VMEM_LIMIT)


def _pick_tm(rows, tm_max):
    k = 1
    while rows % k or (rows // k) > tm_max or (rows // k) % 16:
        k += 1
    return rows // k


def _gelu(x):
    return 0.5 * x * (1.0 + jnp.tanh(0.7978845608028654 * (x + 0.044715 * (x * x * x))))


def _sigmoid(x):
    return 1.0 / (1.0 + jnp.exp(-x))


def _silu(x):
    return x * _sigmoid(x)


def _mod_kernel(a_ref, w_ref, b_ref, o_ref):
    a = _silu(a_ref[...]).astype(BF16)
    o_ref[...] = jnp.dot(a, w_ref[...].astype(BF16), preferred_element_type=F32) + b_ref[...]


def _mod_all(cvec, w_mod, b_mod):
    depth, d, nm = w_mod.shape
    tn = 1024 if nm % 1024 == 0 else nm
    return pl.pallas_call(
        _mod_kernel,
        grid=(depth, nm // tn),
        in_specs=[pl.BlockSpec((SUB, d), lambda l, j: (0, 0)),
                  pl.BlockSpec((None, d, tn), lambda l, j: (l, 0, j)),
                  pl.BlockSpec((None, 1, tn), lambda l, j: (l, 0, j))],
        out_specs=pl.BlockSpec((None, SUB, tn), lambda l, j: (l, 0, j)),
        out_shape=jax.ShapeDtypeStruct((depth, SUB, nm), F32),
        compiler_params=_cp("arbitrary", "arbitrary"),
        name="mod_vectors",
    )(cvec, w_mod, b_mod.reshape(depth, 1, nm))


def _norm_mod_kernel(x_ref, g_ref, sh_ref, sc_ref, o_ref):
    x = x_ref[...]
    y = x * lax.rsqrt(jnp.mean(x * x, axis=-1, keepdims=True) + EPS) * g_ref[...]
    o_ref[...] = (y * (1.0 + sc_ref[...]) + sh_ref[...]).astype(o_ref.dtype)


def _seg_of_block(i, lat_blocks, blocks_per_batch):
    return jnp.where(i < lat_blocks, i // blocks_per_batch, lat_blocks // blocks_per_batch)


def _norm_mod(x, g, modt, l, k_shift, k_scale, lat_blocks, bpb):
    rows, d = x.shape

    def mod_spec(k):
        return pl.BlockSpec((None, None, 1, d),
                            lambda i: (l, _seg_of_block(i, lat_blocks, bpb) * N_MOD + k, 0, 0))

    return pl.pallas_call(
        _norm_mod_kernel,
        grid=(rows // SEG,),
        in_specs=[pl.BlockSpec((SEG, d), lambda i: (i, 0)),
                  pl.BlockSpec((None, 1, d), lambda i: (l, 0, 0)),
                  mod_spec(k_shift), mod_spec(k_scale)],
        out_specs=pl.BlockSpec((SEG, d), lambda i: (i, 0)),
        out_shape=jax.ShapeDtypeStruct((rows, d), BF16),
        compiler_params=_cp("arbitrary"),
        name="norm_modulate",
    )(x, g, modt, modt)


def _mm_kernel(a_ref, w_ref, b_ref, o_ref, *, epilogue):
    acc = jnp.dot(a_ref[...], w_ref[...].astype(BF16), preferred_element_type=F32) + b_ref[...]
    o_ref[...] = epilogue(acc).astype(o_ref.dtype)


def _mm_glu_kernel(a_ref, wp_ref, wq_ref, bp_ref, bq_ref, o_ref):
    a = a_ref[...]
    p = jnp.dot(a, wp_ref[...].astype(BF16), preferred_element_type=F32) + bp_ref[...]
    q = jnp.dot(a, wq_ref[...].astype(BF16), preferred_element_type=F32) + bq_ref[...]
    o_ref[...] = (p * _sigmoid(q)).astype(o_ref.dtype)


def _mm_nobias_kernel(a_ref, w_ref, o_ref):
    o_ref[...] = jnp.dot(a_ref[...], w_ref[...].astype(BF16),
                         preferred_element_type=F32).astype(o_ref.dtype)


def _mm(a, w, b, l, n0, n_len, epilogue, out_dtype):
    rows, k = a.shape
    tm = _pick_tm(rows, MM_TM_MAX)
    tn = MM_TN
    j0 = n0 // tn
    return pl.pallas_call(
        functools.partial(_mm_kernel, epilogue=epilogue),
        grid=(rows // tm, n_len // tn),
        in_specs=[pl.BlockSpec((tm, k), lambda i, j: (i, 0), pipeline_mode=pl.Buffered(1)),
                  pl.BlockSpec((None, k, tn), lambda i, j: (l, 0, j0 + j)),
                  pl.BlockSpec((None, 1, tn), lambda i, j: (l, 0, j0 + j))],
        out_specs=pl.BlockSpec((tm, tn), lambda i, j: (i, j)),
        out_shape=jax.ShapeDtypeStruct((rows, n_len), out_dtype),
        compiler_params=_cp("arbitrary", "arbitrary"),
        name="matmul_bias_act",
    )(a, w, b)


def _mm_glu(a, w, b, l, n0, n_half):
    rows, k = a.shape
    tm = _pick_tm(rows, MM_TM_MAX)
    tn = MM_TN
    jp = n0 // tn
    jq = (n0 + n_half) // tn
    return pl.pallas_call(
        _mm_glu_kernel,
        grid=(rows // tm, n_half // tn),
        in_specs=[pl.BlockSpec((tm, k), lambda i, j: (i, 0), pipeline_mode=pl.Buffered(1)),
                  pl.BlockSpec((None, k, tn), lambda i, j: (l, 0, jp + j)),
                  pl.BlockSpec((None, k, tn), lambda i, j: (l, 0, jq + j)),
                  pl.BlockSpec((None, 1, tn), lambda i, j: (l, 0, jp + j)),
                  pl.BlockSpec((None, 1, tn), lambda i, j: (l, 0, jq + j))],
        out_specs=pl.BlockSpec((tm, tn), lambda i, j: (i, j)),
        out_shape=jax.ShapeDtypeStruct((rows, n_half), BF16),
        compiler_params=_cp("arbitrary", "arbitrary"),
        name="matmul_glu",
    )(a, w, w, b, b)


def _mm_out(a, w, l):
    rows, k = a.shape
    n = w.shape[-1]
    tm = _pick_tm(rows, MM_TM_MAX)
    tn = MM_TN
    return pl.pallas_call(
        _mm_nobias_kernel,
        grid=(rows // tm, n // tn),
        in_specs=[pl.BlockSpec((tm, k), lambda i, j: (i, 0), pipeline_mode=pl.Buffered(1)),
                  pl.BlockSpec((None, k, tn), lambda i, j: (l, 0, j))],
        out_specs=pl.BlockSpec((tm, tn), lambda i, j: (i, j)),
        out_shape=jax.ShapeDtypeStruct((rows, n), F32),
        compiler_params=_cp("arbitrary", "arbitrary"),
        name="matmul_out",
    )(a, w)


def _merge_kernel(ya_ref, yb_ref, yc_ref, wa_ref, wb_ref, wc_ref, ga_ref, gb_ref, gc_ref, o_ref):
    def branch(y_ref, w_ref, g_ref):
        return g_ref[...].astype(F32) * jnp.dot(y_ref[...], w_ref[...].astype(BF16),
                                                preferred_element_type=F32)

    acc = branch(ya_ref, wa_ref, ga_ref) + branch(yb_ref, wb_ref, gb_ref) + branch(yc_ref, wc_ref, gc_ref)
    o_ref[...] = acc.astype(o_ref.dtype)


def _merge(ya, yb, yc, wpa, wpb, wpc, gates, l):
    rows, k = ya.shape
    d = wpa.shape[-1]
    tm = _pick_tm(rows, MM_TM_MAX // 2)
    tn = MM_TN
    nd = d // tn
    a_spec = pl.BlockSpec((tm, k), lambda i, j: (i, 0), pipeline_mode=pl.Buffered(1))
    w_spec = pl.BlockSpec((None, k, tn), lambda i, j: (l, 0, j))

    def g_spec(br):
        return pl.BlockSpec((tm, tn), lambda i, j: (i, br * nd + j))

    return pl.pallas_call(
        _merge_kernel,
        grid=(rows // tm, nd),
        in_specs=[a_spec, a_spec, a_spec, w_spec, w_spec, w_spec, g_spec(0), g_spec(1), g_spec(2)],
        out_specs=pl.BlockSpec((tm, tn), lambda i, j: (i, j)),
        out_shape=jax.ShapeDtypeStruct((rows, d), BF16),
        compiler_params=_cp("arbitrary", "arbitrary"),
        name="merge_branches",
    )(ya, yb, yc, wpa, wpb, wpc, gates, gates, gates)


def _rglru_coeffs(x_ref, prev_ref, next_ref, cw_ref, cb_ref, wr_ref, br_ref, wi_ref, bi_ref, lam_ref,
                  a_scr, b_scr, r_scr, i_scr, first, last):
    tc, w = a_scr.shape
    hd = w // H_A
    x = x_ref[...].astype(F32)
    row = lax.broadcasted_iota(jnp.int32, (tc, w), 0)
    zero_row = jnp.zeros((1, w), F32)
    prev = prev_ref[...].astype(F32)
    nxt = next_ref[...].astype(F32)
    p_last = jnp.where(first, zero_row, prev[15:16, :])
    n0 = jnp.where(last, zero_row, nxt[0:1, :])
    n1 = jnp.where(last, zero_row, nxt[1:2, :])
    x_m1 = jnp.where(row == 0, p_last, pltpu.roll(x, 1, axis=0))
    x_p1 = jnp.where(row == tc - 1, n0, pltpu.roll(x, tc - 1, axis=0))
    x_p2 = jnp.where(row == tc - 2, n0, jnp.where(row == tc - 1, n1, pltpu.roll(x, tc - 2, axis=0)))
    xc = (cw_ref[0:1, :] * x_m1 + cw_ref[1:2, :] * x + cw_ref[2:3, :] * x_p1
          + cw_ref[3:4, :] * x_p2 + cb_ref[...])
    xb = xc.astype(BF16)
    for h in range(H_A):
        sl = slice(h * hd, (h + 1) * hd)
        xh = xb[:, sl]
        r_scr[:, sl] = jnp.dot(xh, wr_ref[h].astype(BF16), preferred_element_type=F32)
        i_scr[:, sl] = jnp.dot(xh, wi_ref[h].astype(BF16), preferred_element_type=F32)
    r = _sigmoid(r_scr[...] + br_ref[...])
    ig = _sigmoid(i_scr[...] + bi_ref[...])
    nl = -lam_ref[...]
    softplus = jnp.maximum(nl, 0.0) + jnp.log1p(jnp.exp(-jnp.abs(nl)))
    log_a = (-LRU_C) * r * softplus
    a = jnp.exp(log_a)
    a_scr[...] = a
    u = a * a
    x2 = 2.0 * log_a
    one_m = jnp.where(u == 1.0, -x2, jnp.where(u == 0.0, 1.0, (1.0 - u) * x2 / jnp.log(u)))
    b_scr[...] = jnp.sqrt(one_m) * (ig * xc)


def _block_scan(a, b, reverse):
    row = lax.broadcasted_iota(jnp.int32, a.shape, 0)
    for d in (1, 2, 4):
        if reverse:
            keep = row < SUB - d
            shift = SUB - d
        else:
            keep = row >= d
            shift = d
        a_sh = jnp.where(keep, pltpu.roll(a, shift, axis=0), 1.0)
        b_sh = jnp.where(keep, pltpu.roll(b, shift, axis=0), 0.0)
        b = a * b_sh + b
        a = a * a_sh
    return a, b


def _rglru_scan(a_scr, b_scr, h_scr, emit, reverse):
    tc, w = a_scr.shape
    nblk = tc // SUB

    def body(k, h):
        kk = (nblk - 1 - k) if reverse else k
        rows = pl.ds(pl.multiple_of(kk * SUB, SUB), SUB)
        ap, bp = _block_scan(a_scr[rows, :], b_scr[rows, :], reverse)
        out = ap * h + bp
        emit(rows, out)
        edge = out[0:1, :] if reverse else out[SUB - 1:SUB, :]
        return jnp.broadcast_to(edge, (SUB, w))

    h_scr[...] = lax.fori_loop(0, nblk, body, h_scr[...])


def _rglru_fwd_kernel(x_ref, prev_ref, next_ref, cw_ref, cb_ref, wr_ref, br_ref, wi_ref, bi_ref, lam_ref,
                      hf_ref, a_scr, b_scr, r_scr, i_scr, h_scr, *, bpb):
    j = pl.program_id(1)

    @pl.when(j == 0)
    def _():
        h_scr[...] = jnp.zeros_like(h_scr)

    first = jnp.logical_or(j == 0, j == 1)
    last = jnp.logical_or(j == 0, j == bpb)
    _rglru_coeffs(x_ref, prev_ref, next_ref, cw_ref, cb_ref, wr_ref, br_ref, wi_ref, bi_ref, lam_ref,
                  a_scr, b_scr, r_scr, i_scr, first, last)

    def emit(rows, out):
        hf_ref[rows, :] = out

    _rglru_scan(a_scr, b_scr, h_scr, emit, reverse=False)


def _rglru_bwd_kernel(x_ref, prev_ref, next_ref, cw_ref, cb_ref, wr_ref, br_ref, wi_ref, bi_ref, lam_ref,
                      hf_ref, ga_ref, ya_ref, a_scr, b_scr, r_scr, i_scr, h_scr, *, bpb):
    j = pl.program_id(1)

    @pl.when(j == 0)
    def _():
        h_scr[...] = jnp.zeros_like(h_scr)

    first = jnp.logical_or(j == 0, j == bpb)
    last = jnp.logical_or(j == 0, j == 1)
    _rglru_coeffs(x_ref, prev_ref, next_ref, cw_ref, cb_ref, wr_ref, br_ref, wi_ref, bi_ref, lam_ref,
                  a_scr, b_scr, r_scr, i_scr, first, last)

    def emit(rows, out):
        ya_ref[rows, :] = ((hf_ref[rows, :] + out) * ga_ref[rows, :].astype(F32)).astype(ya_ref.dtype)

    _rglru_scan(a_scr, b_scr, h_scr, emit, reverse=True)


def _rglru(xa, ga, p, l, nbatch, bpb):
    rows, w = xa.shape
    hd = w // H_A
    lat_blocks = nbatch * bpb
    nhalo = rows // 16
    per_blk = SEG // 16

    def common_specs(direction, blk):
        def prev_map(b, j):
            return (jnp.maximum(blk(b, j) * per_blk - 1, 0), 0)

        def next_map(b, j):
            return (jnp.minimum((blk(b, j) + 1) * per_blk, nhalo - 1), 0)

        vec = pl.BlockSpec((None, None, 1, w), lambda b, j: (l, direction, 0, 0))
        gw = pl.BlockSpec((None, None, H_A, hd, hd), lambda b, j: (l, direction, 0, 0, 0))
        return [pl.BlockSpec((SEG, w), lambda b, j: (blk(b, j), 0)),
                pl.BlockSpec((16, w), prev_map),
                pl.BlockSpec((16, w), next_map),
                pl.BlockSpec((None, CONV_A, w), lambda b, j: (l, 0, 0)),
                pl.BlockSpec((None, 1, w), lambda b, j: (l, 0, 0)),
                gw, vec, gw, vec, vec]

    def blk_f(b, j):
        return jnp.where(j == 0, lat_blocks + b, b * bpb + j - 1)

    def blk_b(b, j):
        return jnp.where(j == 0, lat_blocks + b, b * bpb + bpb - j)

    scratch = [pltpu.VMEM((SEG, w), F32)] * 4 + [pltpu.VMEM((SUB, w), F32)]
    weights = (p["rg_conv_w"], p["rg_conv_b"], p["rg_w_r"], p["rg_b_r"], p["rg_w_i"], p["rg_b_i"], p["rg_lambda"])
    hf = pl.pallas_call(
        functools.partial(_rglru_fwd_kernel, bpb=bpb),
        grid=(nbatch, bpb + 1),
        in_specs=common_specs(0, blk_f),
        out_specs=pl.BlockSpec((SEG, w), lambda b, j: (blk_f(b, j), 0)),
        out_shape=jax.ShapeDtypeStruct((rows, w), F32),
        scratch_shapes=scratch,
        compiler_params=_cp("arbitrary", "arbitrary"),
        name="rglru_forward",
    )(xa, xa, xa, *weights)
    return pl.pallas_call(
        functools.partial(_rglru_bwd_kernel, bpb=bpb),
        grid=(nbatch, bpb + 1),
        in_specs=common_specs(1, blk_b) + [pl.BlockSpec((SEG, w), lambda b, j: (blk_b(b, j), 0)),
                                           pl.BlockSpec((SEG, w), lambda b, j: (blk_b(b, j), 0))],
        out_specs=pl.BlockSpec((SEG, w), lambda b, j: (blk_b(b, j), 0)),
        out_shape=jax.ShapeDtypeStruct((rows, w), BF16),
        scratch_shapes=scratch,
        compiler_params=_cp("arbitrary", "arbitrary"),
        name="rglru_backward",
    )(xa, xa, xa, *weights, hf, ga)


def _layernorm(x, g, b):
    mu = jnp.mean(x, axis=-1, keepdims=True)
    xc = x - mu
    var = jnp.mean(xc * xc, axis=-1, keepdims=True)
    return xc * lax.rsqrt(var + EPS) * g + b


def _sgu_kernel(u_ref, v_ref, g_ref, b_ref, ws_ref, bs_ref, o_ref):
    tc, w = u_ref.shape
    gw = w // G_B
    v = _layernorm(v_ref[...].astype(F32), g_ref[...], b_ref[...]).astype(BF16)
    for n in range(tc // CHUNK):
        rs = slice(n * CHUNK, (n + 1) * CHUNK)
        for g in range(G_B):
            cs = slice(g * gw, (g + 1) * gw)
            s = jnp.dot(ws_ref[g].astype(BF16), v[rs, cs], preferred_element_type=F32)
            o_ref[rs, cs] = (u_ref[rs, cs].astype(F32) * (s + bs_ref[:, cs])).astype(o_ref.dtype)


def _sgu(uv, p, l):
    rows, w2 = uv.shape
    w = w2 // 2
    depth = p["sg_b"].shape[0]
    bs = jnp.repeat(jnp.swapaxes(p["sg_b"], 1, 2), w // G_B, axis=2)
    vec = pl.BlockSpec((None, 1, w), lambda i: (l, 0, 0))
    return pl.pallas_call(
        _sgu_kernel,
        grid=(rows // SEG,),
        in_specs=[pl.BlockSpec((SEG, w), lambda i: (i, 0)),
                  pl.BlockSpec((SEG, w), lambda i: (i, 1)),
                  vec, vec,
                  pl.BlockSpec((None, G_B, CHUNK, CHUNK), lambda i: (l, 0, 0, 0)),
                  pl.BlockSpec((None, CHUNK, w), lambda i: (l, 0, 0))],
        out_specs=pl.BlockSpec((SEG, w), lambda i: (i, 0)),
        out_shape=jax.ShapeDtypeStruct((rows, w), BF16),
        compiler_params=_cp("arbitrary"),
        name="spatial_gating",
    )(uv, uv, p["sg_ln_g"].reshape(depth, 1, w), p["sg_ln_b"].reshape(depth, 1, w), p["sg_w"], bs)


def _conv_lat_kernel(x_ref, cw_ref, cb_ref, g_ref, b_ref, o_ref, xp):
    nr, wc, c = x_ref.shape
    half = CONV_C // 2
    zeros = jnp.zeros((half, wc, c), F32)
    xp[0:half] = zeros
    xp[half + nr:half + nr + half] = zeros
    xp[half:half + nr] = x_ref[...].astype(F32)

    def body(r, carry):
        acc = cw_ref[0:1, :] * xp[r]
        for k in range(1, CONV_C):
            acc = acc + cw_ref[k:k + 1, :] * xp[r + k]
        y = _layernorm(acc + cb_ref[...], g_ref[...], b_ref[...])
        o_ref[r] = _silu(y).astype(o_ref.dtype)
        return carry

    lax.fori_loop(0, nr, body, 0)


def _conv_ctx_kernel(x_ref, cw_ref, cb_ref, g_ref, b_ref, o_ref, xp):
    n, c = x_ref.shape
    half = CONV_C // 2
    pad = 16
    xp[0:pad, :] = jnp.zeros((pad, c), F32)
    xp[pad + n:pad + n + pad, :] = jnp.zeros((pad, c), F32)
    xp[pad:pad + n, :] = x_ref[...].astype(F32)
    acc = cw_ref[0:1, :] * xp[pad - half:pad - half + n, :]
    for k in range(1, CONV_C):
        off = pad - half + k
        acc = acc + cw_ref[k:k + 1, :] * xp[off:off + n, :]
    y = _layernorm(acc + cb_ref[...], g_ref[...], b_ref[...])
    o_ref[...] = _silu(y).astype(o_ref.dtype)


def _conformer(glu, p, l, nbatch, seq, ctx_len):
    rows, c = glu.shape
    depth = p["cv_b"].shape[0]
    nrow = seq // GRID_W
    wt = 16
    cw = p["cv_w"]
    vecs = [p["cv_b"].reshape(depth, 1, c), p["cv_ln_g"].reshape(depth, 1, c), p["cv_ln_b"].reshape(depth, 1, c)]
    g3 = glu.reshape(rows // GRID_W, GRID_W, c)
    half = CONV_C // 2

    lat = pl.pallas_call(
        _conv_lat_kernel,
        grid=(nbatch, GRID_W // wt),
        in_specs=[pl.BlockSpec((nrow, wt, c), lambda b, j: (b, j, 0)),
                  pl.BlockSpec((None, CONV_C, c), lambda b, j: (l, 0, 0))]
                 + [pl.BlockSpec((None, 1, c), lambda b, j: (l, 0, 0))] * 3,
        out_specs=pl.BlockSpec((nrow, wt, c), lambda b, j: (b, j, 0)),
        out_shape=jax.ShapeDtypeStruct(g3.shape, BF16),
        scratch_shapes=[pltpu.VMEM((nrow + 2 * half, wt, c), F32)],
        input_output_aliases={0: 0},
        compiler_params=_cp("arbitrary", "arbitrary"),
        name="conformer_conv_latent",
    )(g3, cw, *vecs)
    lat = lat.reshape(rows, c)
    cblk0 = nbatch * seq // ctx_len
    return pl.pallas_call(
        _conv_ctx_kernel,
        grid=(nbatch,),
        in_specs=[pl.BlockSpec((ctx_len, c), lambda b: (cblk0 + b, 0)),
                  pl.BlockSpec((None, CONV_C, c), lambda b: (l, 0, 0))]
                 + [pl.BlockSpec((None, 1, c), lambda b: (l, 0, 0))] * 3,
        out_specs=pl.BlockSpec((ctx_len, c), lambda b: (cblk0 + b, 0)),
        out_shape=jax.ShapeDtypeStruct((rows, c), BF16),
        scratch_shapes=[pltpu.VMEM((ctx_len + 32, c), F32)],
        input_output_aliases={0: 0},
        compiler_params=_cp("arbitrary"),
        name="conformer_conv_context",
    )(lat, cw, *vecs)


def _ffn_pre_kernel(x_ref, o_ref, gm_ref, g_ref, sh_ref, sc_ref, wr_ref, xo_ref, hx_ref, aff_ref):
    d = x_ref.shape[1]
    dh = d // 2
    x = x_ref[...] + gm_ref[...] * o_ref[...]
    xo_ref[...] = x
    y = x * lax.rsqrt(jnp.mean(x * x, axis=-1, keepdims=True) + EPS) * g_ref[...]
    h = y * (1.0 + sc_ref[...]) + sh_ref[...]
    logits = jnp.dot(h, wr_ref[...], preferred_element_type=F32, precision=lax.Precision.HIGHEST)
    lane = lax.broadcasted_iota(jnp.int32, logits.shape, 1)
    valid = lane < N_EXP
    m = jnp.max(jnp.where(valid, logits, -jnp.inf), axis=-1, keepdims=True)
    ex = jnp.where(valid, jnp.exp(logits - m), 0.0)
    aff = ex / jnp.sum(ex, axis=-1, keepdims=True)
    aff_ref[...] = aff
    lo = pltpu.bitcast(h[:, :dh].astype(BF16).astype(F32), jnp.int32)
    hi = pltpu.bitcast(h[:, dh:].astype(BF16).astype(F32), jnp.int32)
    hx_ref[:, :dh] = jnp.bitwise_or(jnp.bitwise_and(hi, jnp.int32(-65536)),
                                    lax.shift_right_logical(lo, jnp.int32(16)))
    hx_ref[:, dh:] = pltpu.bitcast(aff, jnp.int32)


def _ffn_pre(x, o, g, modt, wr_pad, l, lat_blocks, bpb):
    rows, d = x.shape
    blk = pl.BlockSpec((SEG, d), lambda i: (i, 0))

    def mod_spec(k):
        return pl.BlockSpec((None, None, 1, d),
                            lambda i: (l, _seg_of_block(i, lat_blocks, bpb) * N_MOD + k, 0, 0))

    return pl.pallas_call(
        _ffn_pre_kernel,
        grid=(rows // SEG,),
        in_specs=[blk, blk, mod_spec(2),
                  pl.BlockSpec((None, 1, d), lambda i: (l, 0, 0)),
                  mod_spec(3), mod_spec(4),
                  pl.BlockSpec((None, d, LANES), lambda i: (l, 0, 0))],
        out_specs=[blk,
                   pl.BlockSpec((SEG, d // 2 + LANES), lambda i: (i, 0)),
                   pl.BlockSpec((SEG, LANES), lambda i: (i, 0))],
        out_shape=[jax.ShapeDtypeStruct((rows, d), F32),
                   jax.ShapeDtypeStruct((rows, d // 2 + LANES), jnp.int32),
                   jax.ShapeDtypeStruct((rows, LANES), F32)],
        input_output_aliases={0: 0},
        compiler_params=_cp("arbitrary"),
        name="ffn_pre_router",
    )(x, o, modt, g, modt, modt, wr_pad)


def _prefix_mats():
    r = lax.broadcasted_iota(jnp.int32, (LANES, LANES), 0)
    c = lax.broadcasted_iota(jnp.int32, (LANES, LANES), 1)
    incl = jnp.where(r <= c, 1.0, 0.0).astype(BF16)
    strict = jnp.where(c < r, 1.0, 0.0).astype(BF16)
    return incl, strict


def _topk_kernel(a_ref, o_ref, *, cap):
    nexp, nc, _ = a_ref.shape
    incl, strict = _prefix_mats()
    ones_sq = jnp.ones((LANES, LANES), BF16)
    ones8 = jnp.ones((SUB, LANES), BF16)
    lane_c = lax.broadcasted_iota(jnp.int32, (cap, LANES), 1)
    s_col = lax.broadcasted_iota(jnp.int32, (cap, LANES), 0).astype(F32)
    capf = jnp.float32(cap)
    o_ref[...] = jnp.zeros_like(o_ref)

    def total(x):
        return jnp.sum(jnp.sum(x, axis=1, keepdims=True), axis=0, keepdims=True)

    def pad_rows(x):
        if nc == LANES:
            return x
        return jnp.concatenate([x, jnp.zeros((LANES - nc, LANES), x.dtype)], axis=0)

    def prefix(mask_f):
        mp = pad_rows(mask_f).astype(BF16)
        within = jnp.dot(mp, incl, preferred_element_type=F32)
        tot = jnp.dot(mp, ones_sq, preferred_element_type=F32).astype(BF16)
        excl = jnp.dot(strict, tot, preferred_element_type=F32)
        return mp, within, within + excl

    def body(e, carry):
        bits = pltpu.bitcast(a_ref[e], jnp.int32)
        thr = jnp.zeros((1, 1), jnp.int32)
        for bit in range(30, -1, -1):
            cand = thr | jnp.int32(1 << bit)
            cnt = total(jnp.where(bits >= cand, 1.0, 0.0))
            thr = jnp.where(cnt >= capf, cand, thr)
        gt = bits > thr
        eq = bits == thr
        need = capf - total(jnp.where(gt, 1.0, 0.0))
        _, _, pe = prefix(jnp.where(eq, 1.0, 0.0))
        take = jnp.logical_or(gt, jnp.logical_and(eq, pe[:nc] <= need))
        mp, within, _ = prefix(jnp.where(take, 1.0, 0.0))
        s_row = lax.dot_general(ones8, mp, (((1,), (1,)), ((), ())), preferred_element_type=F32)
        pend_row = jnp.dot(s_row.astype(BF16), incl, preferred_element_type=F32)
        pend_b = jnp.broadcast_to(pend_row[0:1, :], (cap, LANES))
        s_b = jnp.broadcast_to(s_row[0:1, :], (cap, LANES))
        before = jnp.logical_and(pend_b <= s_col, lane_c < nc)
        c_s = jnp.sum(jnp.where(before, 1.0, 0.0), axis=1, keepdims=True)
        p_excl = jnp.sum(jnp.where(before, s_b, 0.0), axis=1, keepdims=True)
        sel = jnp.where(lane_c.astype(F32) == c_s, 1.0, 0.0).astype(BF16)
        w_row = jnp.dot(sel, within.astype(BF16), preferred_element_type=F32)
        cnt = jnp.sum(jnp.where(w_row <= s_col - p_excl, 1.0, 0.0), axis=1, keepdims=True)
        idx = (c_s * float(LANES) + cnt).astype(jnp.int32)
        o_ref[...] = jnp.where(lane_c == e, idx, o_ref[...])
        return carry

    lax.fori_loop(0, nexp, body, 0)


def _topk(aff_t, cap):
    nsets, nexp, nc, _ = aff_t.shape
    return pl.pallas_call(
        functools.partial(_topk_kernel, cap=cap),
        grid=(nsets,),
        in_specs=[pl.BlockSpec((None, nexp, nc, LANES), lambda s: (s, 0, 0, 0))],
        out_specs=pl.BlockSpec((None, cap, LANES), lambda s: (s, 0, 0)),
        out_shape=jax.ShapeDtypeStruct((nsets, cap, LANES), jnp.int32),
        compiler_params=_cp("arbitrary"),
        name="expert_choice_topk",
    )(aff_t)


def _expert_kernel(idx_ref, hx_hbm, wg_ref, wu_ref, wd_ref, o_ref, xg, xlo, xhi, hid, gate, sem, *, nf):
    e = pl.program_id(0)
    s = pl.program_id(1)
    nr = xg.shape[0]
    dh = xlo.shape[1]
    tf = wg_ref.shape[1]

    def row_copy(r, row):
        return pltpu.make_async_copy(hx_hbm.at[pl.ds(row, 1)], xg.at[pl.ds(r, 1)], sem)

    @pl.when(s == 0)
    def _():
        def issue(r, c):
            row_copy(r, idx_ref[e * nr + r]).start()
            return c

        lax.fori_loop(0, nr, issue, 0)

        def drain(r, c):
            row_copy(r, 0).wait()
            return c

        lax.fori_loop(0, nr, drain, 0)

        def unpack(k, c):
            rows = pl.ds(pl.multiple_of(k * ROW_CHUNK, ROW_CHUNK), ROW_CHUNK)
            u = xg[rows, :dh]
            xlo[rows, :] = pltpu.bitcast(lax.shift_left(u, jnp.int32(16)), F32).astype(BF16)
            xhi[rows, :] = pltpu.bitcast(jnp.bitwise_and(u, jnp.int32(-65536)), F32).astype(BF16)
            aff = pltpu.bitcast(xg[rows, dh:], F32)
            lane = lax.broadcasted_iota(jnp.int32, aff.shape, 1)
            g = jnp.sum(jnp.where(lane == e, aff, 0.0), axis=1, keepdims=True)
            gate[rows, :] = jnp.broadcast_to(g, aff.shape)
            return c

        lax.fori_loop(0, nr // ROW_CHUNK, unpack, 0)

    @pl.when(s < nf)
    def _():
        def proj(w_ref):
            return (jnp.dot(xlo[...], w_ref[:dh, :].astype(BF16), preferred_element_type=F32)
                    + jnp.dot(xhi[...], w_ref[dh:, :].astype(BF16), preferred_element_type=F32))

        hid[s] = (_silu(proj(wg_ref)) * proj(wu_ref)).astype(BF16)

    @pl.when(s >= nf)
    def _():
        acc = jnp.dot(hid[0], wd_ref[0:tf, :].astype(BF16), preferred_element_type=F32)
        for f in range(1, nf):
            acc = acc + jnp.dot(hid[f], wd_ref[f * tf:(f + 1) * tf, :].astype(BF16),
                                preferred_element_type=F32)
        for k in range(o_ref.shape[1] // LANES):
            cs = slice(k * LANES, (k + 1) * LANES)
            o_ref[:, cs] = acc[:, cs] * gate[...]


def _experts(idx_flat, hx, w_g, w_u, w_d, l, nr):
    d, dexp = w_g.shape[-2:]
    tf = 256 if dexp % 256 == 0 else dexp
    td = 512 if d % 512 == 0 else d
    nf = dexp // tf
    nd = d // td
    wcols = hx.shape[1]
    grid_spec = pltpu.PrefetchScalarGridSpec(
        num_scalar_prefetch=1,
        grid=(N_EXP, nf + nd),
        in_specs=[pl.BlockSpec(memory_space=pl.ANY),
                  pl.BlockSpec((None, None, d, tf), lambda e, s, idx: (l, e, 0, jnp.minimum(s, nf - 1))),
                  pl.BlockSpec((None, None, d, tf), lambda e, s, idx: (l, e, 0, jnp.minimum(s, nf - 1))),
                  pl.BlockSpec((None, None, dexp, td), lambda e, s, idx: (l, e, 0, jnp.maximum(s - nf, 0)))],
        out_specs=pl.BlockSpec((None, nr, td), lambda e, s, idx: (e, 0, jnp.maximum(s - nf, 0))),
        scratch_shapes=[pltpu.VMEM((nr, wcols), jnp.int32),
                        pltpu.VMEM((nr, d // 2), BF16),
                        pltpu.VMEM((nr, d // 2), BF16),
                        pltpu.VMEM((nf, nr, tf), BF16),
                        pltpu.VMEM((nr, LANES), F32),
                        pltpu.SemaphoreType.DMA(())],
    )
    return pl.pallas_call(
        functools.partial(_expert_kernel, nf=nf),
        grid_spec=grid_spec,
        out_shape=jax.ShapeDtypeStruct((N_EXP, nr, d), F32),
        compiler_params=_cp("arbitrary", "arbitrary"),
        name="experts_swiglu",
    )(idx_flat, hx, w_g, w_u, w_d)


def _scatter_kernel(idx_ref, ye_ref, *refs, segs):
    nseg = len(segs)
    gm_refs = refs[:nseg]
    x_in, x_out, buf, rsem, wsem = refs[nseg:]
    del x_in
    e = pl.program_id(0)
    c = pl.program_id(1)
    nr, dc = buf.shape
    col0 = pl.multiple_of(c * dc, LANES)

    def rd(r, row):
        return pltpu.make_async_copy(x_out.at[pl.ds(row, 1), pl.ds(col0, dc)], buf.at[pl.ds(r, 1)], rsem)

    def wr(r, row):
        return pltpu.make_async_copy(buf.at[pl.ds(r, 1)], x_out.at[pl.ds(row, 1), pl.ds(col0, dc)], wsem)

    def each(fn):
        def body(r, carry):
            fn(r)
            return carry

        lax.fori_loop(0, nr, body, 0)

    each(lambda r: rd(r, idx_ref[e * nr + r]).start())
    each(lambda r: rd(r, 0).wait())
    for (r0, r1), gm_ref in zip(segs, gm_refs):
        def add(k, carry, r0=r0, gm_ref=gm_ref):
            rows = pl.ds(pl.multiple_of(r0 + k * ROW_CHUNK, ROW_CHUNK), ROW_CHUNK)
            buf[rows, :] = buf[rows, :] + gm_ref[...] * ye_ref[rows, :]
            return carry

        lax.fori_loop(0, (r1 - r0) // ROW_CHUNK, add, 0)
    each(lambda r: wr(r, idx_ref[e * nr + r]).start())
    each(lambda r: wr(r, 0).wait())


def _scatter_add(idx_flat, ye, x, modt, l, segs, seg_ids):
    nexp, nr, d = ye.shape
    dc = d // 2
    nseg = len(segs)

    def gm_spec(seg):
        return pl.BlockSpec((None, None, 1, dc), lambda e, c, idx: (l, seg * N_MOD + 5, 0, c))

    grid_spec = pltpu.PrefetchScalarGridSpec(
        num_scalar_prefetch=1,
        grid=(nexp, 2),
        in_specs=[pl.BlockSpec((None, nr, dc), lambda e, c, idx: (e, 0, c))]
                 + [gm_spec(s) for s in seg_ids]
                 + [pl.BlockSpec(memory_space=pl.ANY)],
        out_specs=pl.BlockSpec(memory_space=pl.ANY),
        scratch_shapes=[pltpu.VMEM((nr, dc), F32), pltpu.SemaphoreType.DMA(()), pltpu.SemaphoreType.DMA(())],
    )
    return pl.pallas_call(
        functools.partial(_scatter_kernel, segs=segs),
        grid_spec=grid_spec,
        out_shape=jax.ShapeDtypeStruct(x.shape, x.dtype),
        input_output_aliases={2 + nseg: 0},
        compiler_params=_cp("arbitrary", "arbitrary"),
        name="scatter_add_residual",
    )(idx_flat, ye, *([modt] * nseg), x)


def _final_norm_kernel(x_ref, g_ref, o_ref):
    x = x_ref[...]
    o_ref[...] = x * lax.rsqrt(jnp.mean(x * x, axis=-1, keepdims=True) + EPS) * g_ref[...]


def _final_norm(x, g, nrows):
    d = x.shape[1]
    return pl.pallas_call(
        _final_norm_kernel,
        grid=(nrows // SEG,),
        in_specs=[pl.BlockSpec((SEG, d), lambda i: (i, 0)), pl.BlockSpec((1, d), lambda i: (0, 0))],
        out_specs=pl.BlockSpec((SEG, d), lambda i: (i, 0)),
        out_shape=jax.ShapeDtypeStruct((nrows, d), F32),
        compiler_params=_cp("arbitrary"),
        name="final_norm",
    )(x, g.reshape(1, d))


def _route(aff, nbatch, seq, ctx_len):
    lat = nbatch * seq
    cap_l = CAP_FACTOR * seq // N_EXP
    cap_c = CAP_FACTOR * ctx_len // N_EXP
    a_l = aff[:lat, :N_EXP].reshape(nbatch, seq, N_EXP).transpose(0, 2, 1)
    a_l = a_l.reshape(nbatch, N_EXP, seq // LANES, LANES)
    idx_l = _topk(a_l, cap_l)[:, :, :N_EXP]
    idx_l = idx_l + (jnp.arange(nbatch, dtype=jnp.int32) * seq)[:, None, None]
    cpad = SUB * LANES
    a_c = aff[lat:, :N_EXP].reshape(nbatch, ctx_len, N_EXP).transpose(0, 2, 1)
    a_c = jnp.pad(a_c, ((0, 0), (0, 0), (0, cpad - ctx_len)), constant_values=-1.0)
    a_c = a_c.reshape(nbatch, N_EXP, SUB, LANES)
    idx_c = _topk(a_c, cap_c)[:, :, :N_EXP]
    idx_c = idx_c + (lat + jnp.arange(nbatch, dtype=jnp.int32) * ctx_len)[:, None, None]
    idx = jnp.concatenate([idx_l.transpose(2, 0, 1).reshape(N_EXP, nbatch * cap_l),
                           idx_c.transpose(2, 0, 1).reshape(N_EXP, nbatch * cap_c)], axis=1)
    segs = tuple((b * cap_l, (b + 1) * cap_l) for b in range(nbatch))
    segs = segs + ((nbatch * cap_l, nbatch * (cap_l + cap_c)),)
    seg_ids = tuple(range(nbatch)) + (nbatch,)
    return idx.reshape(-1), idx.shape[1], segs, seg_ids


def kernel(x, c, ctx, c_ctx, w_mod, b_mod, g_mix, g_ffn, w_in, b_in, rg_conv_w, rg_conv_b, rg_w_r, rg_b_r, rg_w_i, rg_b_i, rg_lambda, sg_ln_g, sg_ln_b, sg_w, sg_b, cv_w, cv_b, cv_ln_g, cv_ln_b, w_proj_a, w_proj_b, w_proj_c, w_out, w_router, w_e_gate, w_e_up, w_e_down, g_final):
    nbatch, seq, d = x.shape
    ctx_len = ctx.shape[1]
    depth = w_mod.shape[0]
    w_a = rg_conv_w.shape[-1]
    w_b = sg_ln_g.shape[-1]
    w_c = cv_w.shape[-1]
    assert ctx_len == SEG and seq % SEG == 0 and nbatch + 1 <= SUB
    bpb = seq // SEG
    lat_blocks = nbatch * bpb
    lat_rows = nbatch * seq
    off_xa = w_a
    off_uv = 2 * w_a
    off_glu = off_uv + 2 * w_b
    off_gate = off_glu + 2 * w_c

    cvec = jnp.concatenate([c, c_ctx[None], jnp.zeros((SUB - nbatch - 1, d), F32)], axis=0)
    modt = _mod_all(cvec, w_mod, b_mod).reshape(depth, SUB * N_MOD, 1, d)
    xall = jnp.concatenate([x.reshape(lat_rows, d), ctx.reshape(nbatch * ctx_len, d)], axis=0)

    b_in3 = b_in.reshape(depth, 1, -1)
    gmix3 = g_mix.reshape(depth, 1, d)
    gffn3 = g_ffn.reshape(depth, 1, d)
    wr_pad = jnp.pad(w_router, ((0, 0), (0, 0), (0, LANES - N_EXP)))
    rg = {"rg_conv_w": rg_conv_w, "rg_conv_b": rg_conv_b.reshape(depth, 1, w_a),
          "rg_w_r": rg_w_r, "rg_b_r": rg_b_r.reshape(depth, 2, 1, w_a),
          "rg_w_i": rg_w_i, "rg_b_i": rg_b_i.reshape(depth, 2, 1, w_a),
          "rg_lambda": rg_lambda.reshape(depth, 2, 1, w_a)}
    sg = {"sg_ln_g": sg_ln_g, "sg_ln_b": sg_ln_b, "sg_w": sg_w, "sg_b": sg_b}
    cv = {"cv_w": cv_w, "cv_b": cv_b, "cv_ln_g": cv_ln_g, "cv_ln_b": cv_ln_b}

    for l in range(depth):
        h = _norm_mod(xall, gmix3, modt, l, 0, 1, lat_blocks, bpb)
        ga = _mm(h, w_in, b_in3, l, 0, w_a, _gelu, BF16)
        xa = _mm(h, w_in, b_in3, l, off_xa, w_a, lambda z: z, BF16)
        uv = _mm(h, w_in, b_in3, l, off_uv, 2 * w_b, _gelu, BF16)
        glu = _mm_glu(h, w_in, b_in3, l, off_glu, w_c)
        gates = _mm(h, w_in, b_in3, l, off_gate, N_BRANCH * d, _sigmoid, BF16)
        ya = _rglru(xa, ga, rg, l, nbatch, bpb)
        yb = _sgu(uv, sg, l)
        yc = _conformer(glu, cv, l, nbatch, seq, ctx_len)
        y = _merge(ya, yb, yc, w_proj_a, w_proj_b, w_proj_c, gates, l)
        o = _mm_out(y, w_out, l)
        xall, hx, aff = _ffn_pre(xall, o, gffn3, modt, wr_pad, l, lat_blocks, bpb)
        idx_flat, nr, segs, seg_ids = _route(aff, nbatch, seq, ctx_len)
        ye = _experts(idx_flat, hx, w_e_gate, w_e_up, w_e_down, l, nr)
        xall = _scatter_add(idx_flat, ye, xall, modt, l, segs, seg_ids)
    return _final_norm(xall, g_final, lat_rows).reshape(nbatch, seq, d)
```

```python
import functools

import jax
import jax.numpy as jnp
from jax import lax
from jax.experimental import pallas as pl
from jax.experimental.pallas import tpu as pltpu

GRID_W = 64
N_MOD = 6
H_A = 16
CONV_A = 4
LRU_C = 8.0
G_B = 16
CHUNK = 128
CONV_C = 31
N_BRANCH = 3
N_EXP = 16
CAP_FACTOR = 2
EPS = 1e-6

LANES = 128
SUB = 8
SEG = 256
VMEM_LIMIT = 56 * 1024 * 1024
MM_TN = 256
MM_TM_MAX = 2304
ROW_CHUNK = 32
DMA_UNROLL = 8

BF16 = jnp.bfloat16
F32 = jnp.float32


def _cp(*sem):
    return pltpu.CompilerParams(dimension_semantics=sem, vmem_limit_bytes=VMEM_LIMIT)


def _pick_tm(rows, tm_max):
    k = 1
    while rows % k or (rows // k) > tm_max or (rows // k) % 16:
        k += 1
    return rows // k


def _gelu(x):
    return 0.5 * x * (1.0 + jnp.tanh(0.7978845608028654 * (x + 0.044715 * (x * x * x))))


def _sigmoid(x):
    return 1.0 / (1.0 + jnp.exp(-x))


def _silu(x):
    return x * _sigmoid(x)


def _tree_sum(terms):
    while len(terms) > 1:
        terms = [terms[i] + terms[i + 1] for i in range(0, len(terms) - 1, 2)] + (
            [terms[-1]] if len(terms) % 2 else [])
    return terms[0]


def _mod_kernel(a_ref, w_ref, b_ref, o_ref):
    a = _silu(a_ref[...]).astype(BF16)
    o_ref[...] = jnp.dot(a, w_ref[...].astype(BF16), preferred_element_type=F32) + b_ref[...]


def _mod_all(cvec, w_mod, b_mod):
    depth, d, nm = w_mod.shape
    tn = 1024 if nm % 1024 == 0 else nm
    return pl.pallas_call(
        _mod_kernel,
        grid=(depth, nm // tn),
        in_specs=[pl.BlockSpec((SUB, d), lambda l, j: (0, 0)),
                  pl.BlockSpec((None, d, tn), lambda l, j: (l, 0, j)),
                  pl.BlockSpec((None, 1, tn), lambda l, j: (l, 0, j))],
        out_specs=pl.BlockSpec((None, SUB, tn), lambda l, j: (l, 0, j)),
        out_shape=jax.ShapeDtypeStruct((depth, SUB, nm), F32),
        compiler_params=_cp("arbitrary", "arbitrary"),
        name="mod_vectors",
    )(cvec, w_mod, b_mod.reshape(depth, 1, nm))


def _norm_mod_kernel(x_ref, g_ref, sh_ref, sc_ref, o_ref):
    x = x_ref[...]
    y = x * lax.rsqrt(jnp.mean(x * x, axis=-1, keepdims=True) + EPS) * g_ref[...]
    o_ref[...] = (y * (1.0 + sc_ref[...]) + sh_ref[...]).astype(o_ref.dtype)


def _seg_of_block(i, lat_blocks, blocks_per_batch):
    return jnp.where(i < lat_blocks, i // blocks_per_batch, lat_blocks // blocks_per_batch)


def _norm_mod(x, g, modt, l, k_shift, k_scale, lat_blocks, bpb):
    rows, d = x.shape

    def mod_spec(k):
        return pl.BlockSpec((None, None, 1, d),
                            lambda i: (l, _seg_of_block(i, lat_blocks, bpb) * N_MOD + k, 0, 0))

    return pl.pallas_call(
        _norm_mod_kernel,
        grid=(rows // SEG,),
        in_specs=[pl.BlockSpec((SEG, d), lambda i: (i, 0)),
                  pl.BlockSpec((None, 1, d), lambda i: (l, 0, 0)),
                  mod_spec(k_shift), mod_spec(k_scale)],
        out_specs=pl.BlockSpec((SEG, d), lambda i: (i, 0)),
        out_shape=jax.ShapeDtypeStruct((rows, d), BF16),
        compiler_params=_cp("arbitrary"),
        name="norm_modulate",
    )(x, g, modt, modt)


def _mm_kernel(a_ref, w_ref, b_ref, o_ref, *, epilogue):
    acc = jnp.dot(a_ref[...], w_ref[...].astype(BF16), preferred_element_type=F32) + b_ref[...]
    o_ref[...] = epilogue(acc).astype(o_ref.dtype)


def _mm_glu_kernel(a_ref, wp_ref, wq_ref, bp_ref, bq_ref, o_ref):
    a = a_ref[...]
    p = jnp.dot(a, wp_ref[...].astype(BF16), preferred_element_type=F32) + bp_ref[...]
    q = jnp.dot(a, wq_ref[...].astype(BF16), preferred_element_type=F32) + bq_ref[...]
    o_ref[...] = (p * _sigmoid(q)).astype(o_ref.dtype)


def _mm_nobias_kernel(a_ref, w_ref, o_ref):
    o_ref[...] = jnp.dot(a_ref[...], w_ref[...].astype(BF16),
                         preferred_element_type=F32).astype(o_ref.dtype)


def _mm(a, w, b, l, n0, n_len, epilogue, out_dtype, rows):
    k = a.shape[1]
    tm = _pick_tm(rows, MM_TM_MAX)
    tn = MM_TN
    j0 = n0 // tn
    return pl.pallas_call(
        functools.partial(_mm_kernel, epilogue=epilogue),
        grid=(rows // tm, n_len // tn),
        in_specs=[pl.BlockSpec((tm, k), lambda i, j: (i, 0), pipeline_mode=pl.Buffered(1)),
                  pl.BlockSpec((None, k, tn), lambda i, j: (l, 0, j0 + j)),
                  pl.BlockSpec((None, 1, tn), lambda i, j: (l, 0, j0 + j))],
        out_specs=pl.BlockSpec((tm, tn), lambda i, j: (i, j)),
        out_shape=jax.ShapeDtypeStruct((rows, n_len), out_dtype),
        compiler_params=_cp("arbitrary", "arbitrary"),
        name="matmul_bias_act",
    )(a, w, b)


def _mm_glu(a, w, b, l, n0, n_half, rows):
    k = a.shape[1]
    tm = _pick_tm(rows, MM_TM_MAX)
    tn = MM_TN
    jp = n0 // tn
    jq = (n0 + n_half) // tn
    return pl.pallas_call(
        _mm_glu_kernel,
        grid=(rows // tm, n_half // tn),
        in_specs=[pl.BlockSpec((tm, k), lambda i, j: (i, 0), pipeline_mode=pl.Buffered(1)),
                  pl.BlockSpec((None, k, tn), lambda i, j: (l, 0, jp + j)),
                  pl.BlockSpec((None, k, tn), lambda i, j: (l, 0, jq + j)),
                  pl.BlockSpec((None, 1, tn), lambda i, j: (l, 0, jp + j)),
                  pl.BlockSpec((None, 1, tn), lambda i, j: (l, 0, jq + j))],
        out_specs=pl.BlockSpec((tm, tn), lambda i, j: (i, j)),
        out_shape=jax.ShapeDtypeStruct((rows, n_half), BF16),
        compiler_params=_cp("arbitrary", "arbitrary"),
        name="matmul_glu",
    )(a, w, w, b, b)


def _mm_out(a, w, l):
    rows, k = a.shape
    n = w.shape[-1]
    assert rows % 16 == 0
    tm = _pick_tm(rows, MM_TM_MAX)
    tn = MM_TN
    return pl.pallas_call(
        _mm_nobias_kernel,
        grid=(rows // tm, n // tn),
        in_specs=[pl.BlockSpec((tm, k), lambda i, j: (i, 0), pipeline_mode=pl.Buffered(1)),
                  pl.BlockSpec((None, k, tn), lambda i, j: (l, 0, j))],
        out_specs=pl.BlockSpec((tm, tn), lambda i, j: (i, j)),
        out_shape=jax.ShapeDtypeStruct((rows, n), F32),
        compiler_params=_cp("arbitrary", "arbitrary"),
        name="matmul_out",
    )(a, w)


def _merge_kernel(ya_ref, yb_ref, yc_ref, wa_ref, wb_ref, wc_ref, ga_ref, gb_ref, gc_ref, o_ref):
    def branch(y_ref, w_ref, g_ref):
        return g_ref[...].astype(F32) * jnp.dot(y_ref[...], w_ref[...].astype(BF16),
                                                preferred_element_type=F32)

    acc = branch(ya_ref, wa_ref, ga_ref) + branch(yb_ref, wb_ref, gb_ref) + branch(yc_ref, wc_ref, gc_ref)
    o_ref[...] = acc.astype(o_ref.dtype)


def _merge(ya, yb, yc, wpa, wpb, wpc, gates, l):
    rows, k = yb.shape
    d = wpa.shape[-1]
    tm = _pick_tm(rows, MM_TM_MAX // 2)
    tn = MM_TN
    nd = d // tn
    a_spec = pl.BlockSpec((tm, k), lambda i, j: (i, 0), pipeline_mode=pl.Buffered(1))
    w_spec = pl.BlockSpec((None, k, tn), lambda i, j: (l, 0, j))

    def g_spec(br):
        return pl.BlockSpec((tm, tn), lambda i, j: (i, br * nd + j))

    return pl.pallas_call(
        _merge_kernel,
        grid=(rows // tm, nd),
        in_specs=[a_spec, a_spec, a_spec, w_spec, w_spec, w_spec, g_spec(0), g_spec(1), g_spec(2)],
        out_specs=pl.BlockSpec((tm, tn), lambda i, j: (i, j)),
        out_shape=jax.ShapeDtypeStruct((rows, d), BF16),
        compiler_params=_cp("arbitrary", "arbitrary"),
        name="merge_branches",
    )(ya, yb, yc, wpa, wpb, wpc, gates, gates, gates)


def _rglru_coeffs(x_ref, prev_ref, next_ref, cw_ref, cb_ref, wr_ref, br_ref, wi_ref, bi_ref, lam_ref,
                  a_scr, b_scr, r_scr, i_scr, first, last):
    tc, w = a_scr.shape
    hd = w // H_A
    x = x_ref[...].astype(F32)
    row = lax.broadcasted_iota(jnp.int32, (tc, w), 0)
    zero_row = jnp.zeros((1, w), F32)
    prev = prev_ref[...].astype(F32)
    nxt = next_ref[...].astype(F32)
    p_last = jnp.where(first, zero_row, prev[15:16, :])
    n0 = jnp.where(last, zero_row, nxt[0:1, :])
    n1 = jnp.where(last, zero_row, nxt[1:2, :])
    x_m1 = jnp.where(row == 0, p_last, pltpu.roll(x, 1, axis=0))
    x_p1 = jnp.where(row == tc - 1, n0, pltpu.roll(x, tc - 1, axis=0))
    x_p2 = jnp.where(row == tc - 2, n0, jnp.where(row == tc - 1, n1, pltpu.roll(x, tc - 2, axis=0)))
    xc = (cw_ref[0:1, :] * x_m1 + cw_ref[1:2, :] * x + cw_ref[2:3, :] * x_p1
          + cw_ref[3:4, :] * x_p2 + cb_ref[...])
    xb = xc.astype(BF16)
    for h in range(H_A):
        sl = slice(h * hd, (h + 1) * hd)
        xh = xb[:, sl]
        r_scr[:, sl] = jnp.dot(xh, wr_ref[h].astype(BF16), preferred_element_type=F32)
        i_scr[:, sl] = jnp.dot(xh, wi_ref[h].astype(BF16), preferred_element_type=F32)
    r = _sigmoid(r_scr[...] + br_ref[...])
    ig = _sigmoid(i_scr[...] + bi_ref[...])
    nl = -lam_ref[...]
    softplus = jnp.maximum(nl, 0.0) + jnp.log1p(jnp.exp(-jnp.abs(nl)))
    log_a = (-LRU_C) * r * softplus
    a = jnp.exp(log_a)
    a_scr[...] = a
    u = a * a
    x2 = 2.0 * log_a
    one_m = jnp.where(u == 1.0, -x2, jnp.where(u == 0.0, 1.0, (1.0 - u) * x2 / jnp.log(u)))
    b_scr[...] = jnp.sqrt(one_m) * (ig * xc)


def _block_scan(a, b, reverse):
    row = lax.broadcasted_iota(jnp.int32, a.shape, 0)
    for d in (1, 2, 4):
        if reverse:
            keep = row < SUB - d
            shift = SUB - d
        else:
            keep = row >= d
            shift = d
        a_sh = jnp.where(keep, pltpu.roll(a, shift, axis=0), 1.0)
        b_sh = jnp.where(keep, pltpu.roll(b, shift, axis=0), 0.0)
        b = a * b_sh + b
        a = a * a_sh
    return a, b


def _rglru_scan(a_scr, b_scr, h_scr, emit, reverse):
    tc, w = a_scr.shape
    nblk = tc // SUB

    def body(k, h):
        kk = (nblk - 1 - k) if reverse else k
        rows = pl.ds(pl.multiple_of(kk * SUB, SUB), SUB)
        ap, bp = _block_scan(a_scr[rows, :], b_scr[rows, :], reverse)
        out = ap * h + bp
        emit(rows, out)
        edge = out[0:1, :] if reverse else out[SUB - 1:SUB, :]
        return jnp.broadcast_to(edge, (SUB, w))

    h_scr[...] = lax.fori_loop(0, nblk, body, h_scr[...])


def _rglru_fwd_kernel(x_ref, prev_ref, next_ref, cw_ref, cb_ref, wr_ref, br_ref, wi_ref, bi_ref, lam_ref,
                      hf_ref, a_scr, b_scr, r_scr, i_scr, h_scr, *, bpb):
    j = pl.program_id(1)

    @pl.when(j == 0)
    def _():
        h_scr[...] = jnp.zeros_like(h_scr)

    first = jnp.logical_or(j == 0, j == 1)
    last = jnp.logical_or(j == 0, j == bpb)
    _rglru_coeffs(x_ref, prev_ref, next_ref, cw_ref, cb_ref, wr_ref, br_ref, wi_ref, bi_ref, lam_ref,
                  a_scr, b_scr, r_scr, i_scr, first, last)

    def emit(rows, out):
        hf_ref[rows, :] = out

    _rglru_scan(a_scr, b_scr, h_scr, emit, reverse=False)


def _rglru_bwd_kernel(x_ref, prev_ref, next_ref, cw_ref, cb_ref, wr_ref, br_ref, wi_ref, bi_ref, lam_ref,
                      hf_ref, ga_ref, ya_ref, a_scr, b_scr, r_scr, i_scr, h_scr, *, bpb):
    j = pl.program_id(1)

    @pl.when(j == 0)
    def _():
        h_scr[...] = jnp.zeros_like(h_scr)

    first = jnp.logical_or(j == 0, j == bpb)
    last = jnp.logical_or(j == 0, j == 1)
    _rglru_coeffs(x_ref, prev_ref, next_ref, cw_ref, cb_ref, wr_ref, br_ref, wi_ref, bi_ref, lam_ref,
                  a_scr, b_scr, r_scr, i_scr, first, last)

    def emit(rows, out):
        ya_ref[rows, :] = ((hf_ref[rows, :] + out) * ga_ref[rows, :].astype(F32)).astype(ya_ref.dtype)

    _rglru_scan(a_scr, b_scr, h_scr, emit, reverse=True)


def _rglru(xa, ga, p, l, nbatch, bpb):
    rows, w = xa.shape
    hd = w // H_A
    lat_blocks = nbatch * bpb
    ga_blocks = ga.shape[0] // SEG
    nhalo = rows // 16
    per_blk = SEG // 16

    def common_specs(direction, blk):
        def prev_map(b, j):
            return (jnp.maximum(blk(b, j) * per_blk - 1, 0), 0)

        def next_map(b, j):
            return (jnp.minimum((blk(b, j) + 1) * per_blk, nhalo - 1), 0)

        vec = pl.BlockSpec((None, None, 1, w), lambda b, j: (l, direction, 0, 0))
        gw = pl.BlockSpec((None, None, H_A, hd, hd), lambda b, j: (l, direction, 0, 0, 0))
        return [pl.BlockSpec((SEG, w), lambda b, j: (blk(b, j), 0)),
                pl.BlockSpec((16, w), prev_map),
                pl.BlockSpec((16, w), next_map),
                pl.BlockSpec((None, CONV_A, w), lambda b, j: (l, 0, 0)),
                pl.BlockSpec((None, 1, w), lambda b, j: (l, 0, 0)),
                gw, vec, gw, vec, vec]

    def blk_f(b, j):
        return jnp.where(j == 0, lat_blocks + b, b * bpb + j - 1)

    def blk_b(b, j):
        return jnp.where(j == 0, lat_blocks + b, b * bpb + bpb - j)

    scratch = [pltpu.VMEM((SEG, w), F32)] * 4 + [pltpu.VMEM((SUB, w), F32)]
    weights = (p["rg_conv_w"], p["rg_conv_b"], p["rg_w_r"], p["rg_b_r"], p["rg_w_i"], p["rg_b_i"], p["rg_lambda"])
    hf = pl.pallas_call(
        functools.partial(_rglru_fwd_kernel, bpb=bpb),
        grid=(nbatch, bpb + 1),
        in_specs=common_specs(0, blk_f),
        out_specs=pl.BlockSpec((SEG, w), lambda b, j: (blk_f(b, j), 0)),
        out_shape=jax.ShapeDtypeStruct((rows, w), F32),
        scratch_shapes=scratch,
        compiler_params=_cp("arbitrary", "arbitrary"),
        name="rglru_forward",
    )(xa, xa, xa, *weights)
    return pl.pallas_call(
        functools.partial(_rglru_bwd_kernel, bpb=bpb),
        grid=(nbatch, bpb + 1),
        in_specs=common_specs(1, blk_b) + [pl.BlockSpec((SEG, w), lambda b, j: (blk_b(b, j), 0)),
                                           pl.BlockSpec((SEG, w), lambda b, j: (jnp.minimum(blk_b(b, j), ga_blocks - 1), 0))],
        out_specs=pl.BlockSpec((SEG, w), lambda b, j: (blk_b(b, j), 0)),
        out_shape=jax.ShapeDtypeStruct((rows, w), BF16),
        scratch_shapes=scratch,
        compiler_params=_cp("arbitrary", "arbitrary"),
        name="rglru_backward",
    )(xa, xa, xa, *weights, hf, ga)


def _layernorm(x, g, b):
    mu = jnp.mean(x, axis=-1, keepdims=True)
    xc = x - mu
    var = jnp.mean(xc * xc, axis=-1, keepdims=True)
    return xc * lax.rsqrt(var + EPS) * g + b


def _sgu_kernel(u_ref, v_ref, g_ref, b_ref, ws_ref, bs_ref, o_ref):
    tc, w = u_ref.shape
    gw = w // G_B
    v = _layernorm(v_ref[...].astype(F32), g_ref[...], b_ref[...]).astype(BF16)
    for n in range(tc // CHUNK):
        rs = slice(n * CHUNK, (n + 1) * CHUNK)
        for g in range(G_B):
            cs = slice(g * gw, (g + 1) * gw)
            s = jnp.dot(ws_ref[g].astype(BF16), v[rs, cs], preferred_element_type=F32)
            o_ref[rs, cs] = (u_ref[rs, cs].astype(F32) * (s + bs_ref[:, cs])).astype(o_ref.dtype)


def _sgu(uv, p, l):
    rows, w2 = uv.shape
    w = w2 // 2
    depth = p["sg_b"].shape[0]
    bs = jnp.repeat(jnp.swapaxes(p["sg_b"], 1, 2), w // G_B, axis=2)
    vec = pl.BlockSpec((None, 1, w), lambda i: (l, 0, 0))
    return pl.pallas_call(
        _sgu_kernel,
        grid=(rows // SEG,),
        in_specs=[pl.BlockSpec((SEG, w), lambda i: (i, 0)),
                  pl.BlockSpec((SEG, w), lambda i: (i, 1)),
                  vec, vec,
                  pl.BlockSpec((None, G_B, CHUNK, CHUNK), lambda i: (l, 0, 0, 0)),
                  pl.BlockSpec((None, CHUNK, w), lambda i: (l, 0, 0))],
        out_specs=pl.BlockSpec((SEG, w), lambda i: (i, 0)),
        out_shape=jax.ShapeDtypeStruct((rows, w), BF16),
        compiler_params=_cp("arbitrary"),
        name="spatial_gating",
    )(uv, uv, p["sg_ln_g"].reshape(depth, 1, w), p["sg_ln_b"].reshape(depth, 1, w), p["sg_w"], bs)


def _conv_lat_kernel(x_ref, cw_ref, cb_ref, g_ref, b_ref, o_ref, xp, ys):
    nr, wc, c = x_ref.shape
    half = CONV_C // 2
    zeros = jnp.zeros((half, wc, c), F32)
    xp[0:half, :, 0:c] = zeros
    xp[half + nr:half + nr + half, :, 0:c] = zeros
    xp[half:half + nr, :, 0:c] = x_ref[...].astype(F32)

    for g in range(c // LANES):
        ls = slice(g * LANES, (g + 1) * LANES)
        wv = [jnp.broadcast_to(cw_ref[k:k + 1, ls], (SUB, LANES)) for k in range(CONV_C)]

        def taps(r, carry, ls=ls, wv=wv):
            for hs in range(wc // SUB):
                ss = slice(hs * SUB, (hs + 1) * SUB)
                ys[r, ss, ls] = _tree_sum([wv[k] * xp[r + k, ss, ls] for k in range(CONV_C)])
            return carry

        lax.fori_loop(0, nr, taps, 0, unroll=2)

    def finish(r, carry):
        y = _layernorm(ys[r] + cb_ref[...], g_ref[...], b_ref[...])
        o_ref[r] = _silu(y).astype(o_ref.dtype)
        return carry

    lax.fori_loop(0, nr, finish, 0, unroll=2)


def _conv_ctx_kernel(x_ref, cw_ref, cb_ref, g_ref, b_ref, o_ref, xp):
    n, c = x_ref.shape
    half = CONV_C // 2
    pad = 16
    xp[0:pad, :] = jnp.zeros((pad, c), F32)
    xp[pad + n:pad + n + pad, :] = jnp.zeros((pad, c), F32)
    xp[pad:pad + n, :] = x_ref[...].astype(F32)
    acc = cw_ref[0:1, :] * xp[pad - half:pad - half + n, :]
    for k in range(1, CONV_C):
        off = pad - half + k
        acc = acc + cw_ref[k:k + 1, :] * xp[off:off + n, :]
    y = _layernorm(acc + cb_ref[...], g_ref[...], b_ref[...])
    o_ref[...] = _silu(y).astype(o_ref.dtype)


def _conformer(glu, p, l, nbatch, seq, ctx_len, with_ctx):
    rows, c = glu.shape
    depth = p["cv_b"].shape[0]
    nrow = seq // GRID_W
    wt = 16
    cw = p["cv_w"]
    vecs = [p["cv_b"].reshape(depth, 1, c), p["cv_ln_g"].reshape(depth, 1, c), p["cv_ln_b"].reshape(depth, 1, c)]
    g3 = glu.reshape(rows // GRID_W, GRID_W, c)
    half = CONV_C // 2

    lat = pl.pallas_call(
        _conv_lat_kernel,
        grid=(nbatch, GRID_W // wt),
        in_specs=[pl.BlockSpec((nrow, wt, c), lambda b, j: (b, j, 0)),
                  pl.BlockSpec((None, CONV_C, c), lambda b, j: (l, 0, 0))]
                 + [pl.BlockSpec((None, 1, c), lambda b, j: (l, 0, 0))] * 3,
        out_specs=pl.BlockSpec((nrow, wt, c), lambda b, j: (b, j, 0)),
        out_shape=jax.ShapeDtypeStruct((nbatch * nrow, GRID_W, c), BF16),
        scratch_shapes=[pltpu.VMEM((nrow + 2 * half, wt, c + LANES), F32), pltpu.VMEM((nrow, wt, c), F32)],
        compiler_params=_cp("arbitrary", "arbitrary"),
        name="conformer_conv_latent",
    )(g3, cw, *vecs)
    lat = lat.reshape(nbatch * seq, c)
    if not with_ctx:
        return lat
    cblk0 = nbatch * seq // ctx_len
    cx = pl.pallas_call(
        _conv_ctx_kernel,
        grid=(nbatch,),
        in_specs=[pl.BlockSpec((ctx_len, c), lambda b: (cblk0 + b, 0)),
                  pl.BlockSpec((None, CONV_C, c), lambda b: (l, 0, 0))]
                 + [pl.BlockSpec((None, 1, c), lambda b: (l, 0, 0))] * 3,
        out_specs=pl.BlockSpec((ctx_len, c), lambda b: (b, 0)),
        out_shape=jax.ShapeDtypeStruct((nbatch * ctx_len, c), BF16),
        scratch_shapes=[pltpu.VMEM((ctx_len + 32, c), F32)],
        compiler_params=_cp("arbitrary"),
        name="conformer_conv_context",
    )(glu, cw, *vecs)
    return jnp.concatenate([lat, cx], axis=0)


def _ffn_pre_kernel(x_ref, o_ref, gm_ref, g_ref, sh_ref, sc_ref, wr_ref, xo_ref, hx_ref, aff_ref):
    d = x_ref.shape[1]
    dh = d // 2
    x = x_ref[...] + gm_ref[...] * o_ref[...]
    xo_ref[...] = x
    y = x * lax.rsqrt(jnp.mean(x * x, axis=-1, keepdims=True) + EPS) * g_ref[...]
    h = y * (1.0 + sc_ref[...]) + sh_ref[...]
    w = wr_ref[...]
    h_hi = h.astype(BF16)
    h_lo = (h - h_hi.astype(F32)).astype(BF16)
    w_hi = w.astype(BF16)
    w_lo = (w - w_hi.astype(F32)).astype(BF16)
    logits = (jnp.dot(h_hi, w_hi, preferred_element_type=F32)
              + (jnp.dot(h_lo, w_hi, preferred_element_type=F32)
                 + jnp.dot(h_hi, w_lo, preferred_element_type=F32)))
    lane = lax.broadcasted_iota(jnp.int32, logits.shape, 1)
    valid = lane < N_EXP
    m = jnp.max(jnp.where(valid, logits, -jnp.inf), axis=-1, keepdims=True)
    ex = jnp.where(valid, jnp.exp(logits - m), 0.0)
    aff = ex / jnp.sum(ex, axis=-1, keepdims=True)
    aff_ref[...] = aff
    lo = pltpu.bitcast(h[:, :dh].astype(BF16).astype(F32), jnp.int32)
    hi = pltpu.bitcast(h[:, dh:].astype(BF16).astype(F32), jnp.int32)
    hx_ref[:, :dh] = jnp.bitwise_or(jnp.bitwise_and(hi, jnp.int32(-65536)),
                                    lax.shift_right_logical(lo, jnp.int32(16)))
    hx_ref[:, dh:] = pltpu.bitcast(aff, jnp.int32)


def _ffn_pre(x, o, g, modt, wr_pad, l, lat_blocks, bpb):
    rows = o.shape[0]
    d = x.shape[1]
    blk = pl.BlockSpec((SEG, d), lambda i: (i, 0))

    def mod_spec(k):
        return pl.BlockSpec((None, None, 1, d),
                            lambda i: (l, _seg_of_block(i, lat_blocks, bpb) * N_MOD + k, 0, 0))

    return pl.pallas_call(
        _ffn_pre_kernel,
        grid=(rows // SEG,),
        in_specs=[blk, blk, mod_spec(2),
                  pl.BlockSpec((None, 1, d), lambda i: (l, 0, 0)),
                  mod_spec(3), mod_spec(4),
                  pl.BlockSpec((None, d, LANES), lambda i: (l, 0, 0))],
        out_specs=[blk,
                   pl.BlockSpec((SEG, d // 2 + LANES), lambda i: (i, 0)),
                   pl.BlockSpec((SEG, LANES), lambda i: (i, 0))],
        out_shape=[jax.ShapeDtypeStruct(x.shape, F32),
                   jax.ShapeDtypeStruct((rows, d // 2 + LANES), jnp.int32),
                   jax.ShapeDtypeStruct((rows, LANES), F32)],
        input_output_aliases={0: 0},
        compiler_params=_cp("arbitrary"),
        name="ffn_pre_router",
    )(x, o, modt, g, modt, modt, wr_pad)


def _prefix_mats():
    r = lax.broadcasted_iota(jnp.int32, (LANES, LANES), 0)
    c = lax.broadcasted_iota(jnp.int32, (LANES, LANES), 1)
    incl = jnp.where(r <= c, 1.0, 0.0).astype(BF16)
    strict = jnp.where(c < r, 1.0, 0.0).astype(BF16)
    return incl, strict


def _topk_kernel(a_ref, o_ref, thr_scr, *, cap):
    nexp, nc, _ = a_ref.shape
    bits_all = pltpu.bitcast(a_ref[...], jnp.int32)
    thr_all = jnp.zeros((nexp, 1, 1), jnp.int32)
    for bit in range(30, -1, -1):
        cand = thr_all | jnp.int32(1 << bit)
        ge = jnp.where(bits_all >= cand, 1.0, 0.0)
        cnt_all = jnp.sum(jnp.sum(ge, axis=2, keepdims=True), axis=1, keepdims=True)
        thr_all = jnp.where(cnt_all >= float(cap), cand, thr_all)
    thr_scr[...] = jnp.broadcast_to(thr_all, thr_scr.shape)
    incl, strict = _prefix_mats()
    ones_sq = jnp.ones((LANES, LANES), BF16)
    ones8 = jnp.ones((SUB, LANES), BF16)
    lane_c = lax.broadcasted_iota(jnp.int32, (cap, LANES), 1)
    s_col = lax.broadcasted_iota(jnp.int32, (cap, LANES), 0).astype(F32)
    capf = jnp.float32(cap)
    o_ref[...] = jnp.zeros_like(o_ref)

    def total(x):
        return jnp.sum(jnp.sum(x, axis=1, keepdims=True), axis=0, keepdims=True)

    def pad_rows(x):
        if nc == LANES:
            return x
        return jnp.concatenate([x, jnp.zeros((LANES - nc, LANES), x.dtype)], axis=0)

    def prefix(mask_f):
        mp = pad_rows(mask_f).astype(BF16)
        within = jnp.dot(mp, incl, preferred_element_type=F32)
        tot = jnp.dot(mp, ones_sq, preferred_element_type=F32).astype(BF16)
        excl = jnp.dot(strict, tot, preferred_element_type=F32)
        return mp, within, within + excl

    def body(e, carry):
        bits = pltpu.bitcast(a_ref[e], jnp.int32)
        thr = thr_scr[e][0:1, :]
        gt = bits > thr
        eq = bits == thr
        need = capf - total(jnp.where(gt, 1.0, 0.0))
        _, _, pe = prefix(jnp.where(eq, 1.0, 0.0))
        take = jnp.logical_or(gt, jnp.logical_and(eq, pe[:nc] <= need))
        mp, within, _ = prefix(jnp.where(take, 1.0, 0.0))
        s_row = lax.dot_general(ones8, mp, (((1,), (1,)), ((), ())), preferred_element_type=F32)
        pend_row = jnp.dot(s_row.astype(BF16), incl, preferred_element_type=F32)
        pend_b = jnp.broadcast_to(pend_row[0:1, :], (cap, LANES))
        s_b = jnp.broadcast_to(s_row[0:1, :], (cap, LANES))
        before = jnp.logical_and(pend_b <= s_col, lane_c < nc)
        c_s = jnp.sum(jnp.where(before, 1.0, 0.0), axis=1, keepdims=True)
        p_excl = jnp.sum(jnp.where(before, s_b, 0.0), axis=1, keepdims=True)
        sel = jnp.where(lane_c.astype(F32) == c_s, 1.0, 0.0).astype(BF16)
        w_row = jnp.dot(sel, within.astype(BF16), preferred_element_type=F32)
        cnt = jnp.sum(jnp.where(w_row <= s_col - p_excl, 1.0, 0.0), axis=1, keepdims=True)
        idx = (c_s * float(LANES) + cnt).astype(jnp.int32)
        o_ref[...] = jnp.where(lane_c == e, idx, o_ref[...])
        return carry

    lax.fori_loop(0, nexp, body, 0)


def _topk(aff_t, cap):
    nsets, nexp, nc, _ = aff_t.shape
    return pl.pallas_call(
        functools.partial(_topk_kernel, cap=cap),
        grid=(nsets,),
        in_specs=[pl.BlockSpec((None, nexp, nc, LANES), lambda s: (s, 0, 0, 0))],
        out_specs=pl.BlockSpec((None, cap, LANES), lambda s: (s, 0, 0)),
        out_shape=jax.ShapeDtypeStruct((nsets, cap, LANES), jnp.int32),
        scratch_shapes=[pltpu.VMEM((nexp, SUB, LANES), jnp.int32)],
        compiler_params=_cp("arbitrary"),
        name="expert_choice_topk",
    )(aff_t)


def _expert_kernel(idx_ref, hx_hbm, wg_ref, wu_ref, wd_ref, *refs, nf, segs):
    nseg = len(segs)
    gm_refs = refs[:nseg]
    o_ref, xg, xlo, xhi, hid, gate, sem = refs[nseg:]
    e = pl.program_id(0)
    s = pl.program_id(1)
    nr = xg.shape[0]
    dh = xlo.shape[1]
    tf = wg_ref.shape[1]

    def row_copy(r, row):
        return pltpu.make_async_copy(hx_hbm.at[pl.ds(row, 1)], xg.at[pl.ds(r, 1)], sem)

    @pl.when(s == 0)
    def _():
        def issue(r, c):
            row_copy(r, idx_ref[e * nr + r]).start()
            return c

        lax.fori_loop(0, nr, issue, 0, unroll=DMA_UNROLL)

        def drain(r, c):
            row_copy(r, 0).wait()
            return c

        lax.fori_loop(0, nr, drain, 0, unroll=DMA_UNROLL)

        def unpack(k, c):
            rows = pl.ds(pl.multiple_of(k * ROW_CHUNK, ROW_CHUNK), ROW_CHUNK)
            u = xg[rows, :dh]
            xlo[rows, :] = pltpu.bitcast(lax.shift_left(u, jnp.int32(16)), F32).astype(BF16)
            xhi[rows, :] = pltpu.bitcast(jnp.bitwise_and(u, jnp.int32(-65536)), F32).astype(BF16)
            aff = pltpu.bitcast(xg[rows, dh:], F32)
            lane = lax.broadcasted_iota(jnp.int32, aff.shape, 1)
            g = jnp.sum(jnp.where(lane == e, aff, 0.0), axis=1, keepdims=True)
            gate[rows, :] = jnp.broadcast_to(g, aff.shape)
            return c

        lax.fori_loop(0, nr // ROW_CHUNK, unpack, 0)

    @pl.when(s < nf)
    def _():
        def proj(w_ref):
            return (jnp.dot(xlo[...], w_ref[:dh, :].astype(BF16), preferred_element_type=F32)
                    + jnp.dot(xhi[...], w_ref[dh:, :].astype(BF16), preferred_element_type=F32))

        hid[s] = (_silu(proj(wg_ref)) * proj(wu_ref)).astype(BF16)

    @pl.when(s >= nf)
    def _():
        acc = jnp.dot(hid[0], wd_ref[0:tf, :].astype(BF16), preferred_element_type=F32)
        for f in range(1, nf):
            acc = acc + jnp.dot(hid[f], wd_ref[f * tf:(f + 1) * tf, :].astype(BF16),
                                preferred_element_type=F32)
        for (r0, r1), gm_ref in zip(segs, gm_refs):
            for k in range(o_ref.shape[1] // LANES):
                cs = slice(k * LANES, (k + 1) * LANES)
                o_ref[r0:r1, cs] = (acc[r0:r1, cs] * gate[r0:r1, :] * gm_ref[:, cs]).astype(o_ref.dtype)


def _experts(idx_flat, hx, w_g, w_u, w_d, modt, l, nr, segs, seg_ids):
    d, dexp = w_g.shape[-2:]
    tf = 256 if dexp % 256 == 0 else dexp
    td = 512 if d % 512 == 0 else d
    nf = dexp // tf
    nd = d // td
    wcols = hx.shape[1]

    def gm_spec(seg):
        return pl.BlockSpec((None, None, 1, td),
                            lambda e, s, idx: (l, seg * N_MOD + 5, 0, jnp.maximum(s - nf, 0)))

    grid_spec = pltpu.PrefetchScalarGridSpec(
        num_scalar_prefetch=1,
        grid=(N_EXP, nf + nd),
        in_specs=[pl.BlockSpec(memory_space=pl.ANY),
                  pl.BlockSpec((None, None, d, tf), lambda e, s, idx: (l, e, 0, jnp.minimum(s, nf - 1))),
                  pl.BlockSpec((None, None, d, tf), lambda e, s, idx: (l, e, 0, jnp.minimum(s, nf - 1))),
                  pl.BlockSpec((None, None, dexp, td), lambda e, s, idx: (l, e, 0, jnp.maximum(s - nf, 0)))]
                 + [gm_spec(sg) for sg in seg_ids],
        out_specs=pl.BlockSpec((None, nr, td), lambda e, s, idx: (e, 0, jnp.maximum(s - nf, 0))),
        scratch_shapes=[pltpu.VMEM((nr, wcols), jnp.int32),
                        pltpu.VMEM((nr, d // 2), BF16),
                        pltpu.VMEM((nr, d // 2), BF16),
                        pltpu.VMEM((nf, nr, tf), BF16),
                        pltpu.VMEM((nr, LANES), F32),
                        pltpu.SemaphoreType.DMA(())],
    )
    return pl.pallas_call(
        functools.partial(_expert_kernel, nf=nf, segs=segs),
        grid_spec=grid_spec,
        out_shape=jax.ShapeDtypeStruct((N_EXP, nr, d), BF16),
        compiler_params=_cp("arbitrary", "arbitrary"),
        name="experts_swiglu",
    )(idx_flat, hx, w_g, w_u, w_d, *([modt] * len(seg_ids)))


def _scatter_kernel(idx_ref, ye_ref, x_in, x_out, buf, rsem, wsem):
    del x_in
    e = pl.program_id(0)
    nr = buf.shape[0]

    def rd(r, row):
        return pltpu.make_async_copy(x_out.at[pl.ds(row, 1)], buf.at[pl.ds(r, 1)], rsem)

    def wr(r, row):
        return pltpu.make_async_copy(buf.at[pl.ds(r, 1)], x_out.at[pl.ds(row, 1)], wsem)

    def each(fn):
        def body(r, carry):
            fn(r)
            return carry

        lax.fori_loop(0, nr, body, 0, unroll=DMA_UNROLL)

    each(lambda r: rd(r, idx_ref[e * nr + r]).start())
    each(lambda r: rd(r, 0).wait())

    def add(k, carry):
        rows = pl.ds(pl.multiple_of(k * ROW_CHUNK, ROW_CHUNK), ROW_CHUNK)
        buf[rows, :] = buf[rows, :] + ye_ref[rows, :].astype(F32)
        return carry

    lax.fori_loop(0, nr // ROW_CHUNK, add, 0)
    each(lambda r: wr(r, idx_ref[e * nr + r]).start())
    each(lambda r: wr(r, 0).wait())


def _scatter_add(idx_flat, ye, x):
    nexp, nr, d = ye.shape
    grid_spec = pltpu.PrefetchScalarGridSpec(
        num_scalar_prefetch=1,
        grid=(nexp,),
        in_specs=[pl.BlockSpec((None, nr, d), lambda e, idx: (e, 0, 0)),
                  pl.BlockSpec(memory_space=pl.ANY)],
        out_specs=pl.BlockSpec(memory_space=pl.ANY),
        scratch_shapes=[pltpu.VMEM((nr, d), F32), pltpu.SemaphoreType.DMA(()), pltpu.SemaphoreType.DMA(())],
    )
    return pl.pallas_call(
        _scatter_kernel,
        grid_spec=grid_spec,
        out_shape=jax.ShapeDtypeStruct(x.shape, x.dtype),
        input_output_aliases={2: 0},
        compiler_params=_cp("arbitrary"),
        name="scatter_add_residual",
    )(idx_flat, ye, x)


def _final_norm_kernel(x_ref, g_ref, o_ref):
    x = x_ref[...]
    o_ref[...] = x * lax.rsqrt(jnp.mean(x * x, axis=-1, keepdims=True) + EPS) * g_ref[...]


def _final_norm(x, g, nrows):
    d = x.shape[1]
    return pl.pallas_call(
        _final_norm_kernel,
        grid=(nrows // SEG,),
        in_specs=[pl.BlockSpec((SEG, d), lambda i: (i, 0)), pl.BlockSpec((1, d), lambda i: (0, 0))],
        out_specs=pl.BlockSpec((SEG, d), lambda i: (i, 0)),
        out_shape=jax.ShapeDtypeStruct((nrows, d), F32),
        compiler_params=_cp("arbitrary"),
        name="final_norm",
    )(x, g.reshape(1, d))


def _route(aff, nbatch, seq, ctx_len, with_ctx):
    lat = nbatch * seq
    cap_l = CAP_FACTOR * seq // N_EXP
    cap_c = CAP_FACTOR * ctx_len // N_EXP
    a_l = aff[:lat, :N_EXP].reshape(nbatch, seq, N_EXP).transpose(0, 2, 1)
    a_l = a_l.reshape(nbatch, N_EXP, seq // LANES, LANES)
    idx_l = _topk(a_l, cap_l)[:, :, :N_EXP]
    idx_l = idx_l + (jnp.arange(nbatch, dtype=jnp.int32) * seq)[:, None, None]
    segs = tuple((b * cap_l, (b + 1) * cap_l) for b in range(nbatch))
    seg_ids = tuple(range(nbatch))
    if not with_ctx:
        idx = idx_l.transpose(2, 0, 1).reshape(N_EXP, nbatch * cap_l)
        return idx.reshape(-1), idx.shape[1], segs, seg_ids
    cpad = SUB * LANES
    a_c = aff[lat:, :N_EXP].reshape(nbatch, ctx_len, N_EXP).transpose(0, 2, 1)
    a_c = jnp.pad(a_c, ((0, 0), (0, 0), (0, cpad - ctx_len)), constant_values=-1.0)
    a_c = a_c.reshape(nbatch, N_EXP, SUB, LANES)
    idx_c = _topk(a_c, cap_c)[:, :, :N_EXP]
    idx_c = idx_c + (lat + jnp.arange(nbatch, dtype=jnp.int32) * ctx_len)[:, None, None]
    idx = jnp.concatenate([idx_l.transpose(2, 0, 1).reshape(N_EXP, nbatch * cap_l),
                           idx_c.transpose(2, 0, 1).reshape(N_EXP, nbatch * cap_c)], axis=1)
    segs = segs + ((nbatch * cap_l, nbatch * (cap_l + cap_c)),)
    seg_ids = seg_ids + (nbatch,)
    return idx.reshape(-1), idx.shape[1], segs, seg_ids


def kernel(x, c, ctx, c_ctx, w_mod, b_mod, g_mix, g_ffn, w_in, b_in, rg_conv_w, rg_conv_b, rg_w_r, rg_b_r, rg_w_i, rg_b_i, rg_lambda, sg_ln_g, sg_ln_b, sg_w, sg_b, cv_w, cv_b, cv_ln_g, cv_ln_b, w_proj_a, w_proj_b, w_proj_c, w_out, w_router, w_e_gate, w_e_up, w_e_down, g_final):
    nbatch, seq, d = x.shape
    ctx_len = ctx.shape[1]
    depth = w_mod.shape[0]
    w_a = rg_conv_w.shape[-1]
    w_b = sg_ln_g.shape[-1]
    w_c = cv_w.shape[-1]
    assert ctx_len == SEG and seq % SEG == 0 and nbatch + 1 <= SUB
    bpb = seq // SEG
    lat_blocks = nbatch * bpb
    lat_rows = nbatch * seq
    all_rows = lat_rows + nbatch * ctx_len
    off_xa = w_a
    off_uv = 2 * w_a
    off_glu = off_uv + 2 * w_b
    off_gate = off_glu + 2 * w_c

    cvec = jnp.concatenate([c, c_ctx[None], jnp.zeros((SUB - nbatch - 1, d), F32)], axis=0)
    modt = _mod_all(cvec, w_mod, b_mod).reshape(depth, SUB * N_MOD, 1, d)
    xall = jnp.concatenate([x.reshape(lat_rows, d), ctx.reshape(nbatch * ctx_len, d)], axis=0)

    b_in3 = b_in.reshape(depth, 1, -1)
    gmix3 = g_mix.reshape(depth, 1, d)
    gffn3 = g_ffn.reshape(depth, 1, d)
    wr_pad = jnp.pad(w_router, ((0, 0), (0, 0), (0, LANES - N_EXP)))
    rg = {"rg_conv_w": rg_conv_w, "rg_conv_b": rg_conv_b.reshape(depth, 1, w_a),
          "rg_w_r": rg_w_r, "rg_b_r": rg_b_r.reshape(depth, 2, 1, w_a),
          "rg_w_i": rg_w_i, "rg_b_i": rg_b_i.reshape(depth, 2, 1, w_a),
          "rg_lambda": rg_lambda.reshape(depth, 2, 1, w_a)}
    sg = {"sg_ln_g": sg_ln_g, "sg_ln_b": sg_ln_b, "sg_w": sg_w, "sg_b": sg_b}
    cv = {"cv_w": cv_w, "cv_b": cv_b, "cv_ln_g": cv_ln_g, "cv_ln_b": cv_ln_b}

    for l in range(depth):
        need_ctx = l < depth - 1
        nrows = all_rows if need_ctx else lat_rows
        h = _norm_mod(xall, gmix3, modt, l, 0, 1, lat_blocks, bpb)
        ga = _mm(h, w_in, b_in3, l, 0, w_a, _gelu, BF16, nrows)
        xa = _mm(h, w_in, b_in3, l, off_xa, w_a, lambda z: z, BF16, all_rows)
        uv = _mm(h, w_in, b_in3, l, off_uv, 2 * w_b, _gelu, BF16, nrows)
        glu = _mm_glu(h, w_in, b_in3, l, off_glu, w_c, nrows)
        gates = _mm(h, w_in, b_in3, l, off_gate, N_BRANCH * d, _sigmoid, BF16, nrows)
        ya = _rglru(xa, ga, rg, l, nbatch, bpb)
        yb = _sgu(uv, sg, l)
        yc = _conformer(glu, cv, l, nbatch, seq, ctx_len, need_ctx)
        y = _merge(ya, yb, yc, w_proj_a, w_proj_b, w_proj_c, gates, l)
        o = _mm_out(y, w_out, l)
        xall, hx, aff = _ffn_pre(xall, o, gffn3, modt, wr_pad, l, lat_blocks, bpb)
        idx_flat, nr, segs, seg_ids = _route(aff, nbatch, seq, ctx_len, need_ctx)
        ye = _experts(idx_flat, hx, w_e_gate, w_e_up, w_e_down, modt, l, nr, segs, seg_ids)
        xall = _scatter_add(idx_flat, ye, xall)
    return _final_norm(xall, g_final, lat_rows).reshape(nbatch, seq, d)
```

```python
import functools

import jax
import jax.numpy as jnp
from jax import lax
from jax.experimental import pallas as pl
from jax.experimental.pallas import tpu as pltpu

GRID_W = 64
N_MOD = 6
H_A = 16
CONV_A = 4
LRU_C = 8.0
G_B = 16
CHUNK = 128
CONV_C = 31
N_BRANCH = 3
N_EXP = 16
CAP_FACTOR = 2
EPS = 1e-6

LANES = 128
SUB = 8
SEG = 256
VMEM_LIMIT = 56 * 1024 * 1024
MM_TN = 256
MM_TM_MAX = 1152
ROW_CHUNK = 32
DMA_UNROLL = 8

BF16 = jnp.bfloat16
F32 = jnp.float32


def _cp(*sem):
    return pltpu.CompilerParams(dimension_semantics=sem, vmem_limit_bytes=VMEM_LIMIT)


def _pick_tm(rows, tm_max):
    k = 1
    while rows % k or (rows // k) > tm_max or (rows // k) % 16:
        k += 1
    return rows // k


def _gelu(x):
    return 0.5 * x * (1.0 + jnp.tanh(0.7978845608028654 * (x + 0.044715 * (x * x * x))))


def _sigmoid(x):
    return 1.0 / (1.0 + jnp.exp(-x))


def _silu(x):
    return x * _sigmoid(x)


def _tree_sum(terms):
    while len(terms) > 1:
        terms = [terms[i] + terms[i + 1] for i in range(0, len(terms) - 1, 2)] + (
            [terms[-1]] if len(terms) % 2 else [])
    return terms[0]


def _mod_kernel(a_ref, w_ref, b_ref, o_ref):
    a = _silu(a_ref[...]).astype(BF16)
    o_ref[...] = jnp.dot(a, w_ref[...].astype(BF16), preferred_element_type=F32) + b_ref[...]


def _mod_all(cvec, w_mod, b_mod):
    depth, d, nm = w_mod.shape
    tn = 1024 if nm % 1024 == 0 else nm
    return pl.pallas_call(
        _mod_kernel,
        grid=(depth, nm // tn),
        in_specs=[pl.BlockSpec((SUB, d), lambda l, j: (0, 0)),
                  pl.BlockSpec((None, d, tn), lambda l, j: (l, 0, j)),
                  pl.BlockSpec((None, 1, tn), lambda l, j: (l, 0, j))],
        out_specs=pl.BlockSpec((None, SUB, tn), lambda l, j: (l, 0, j)),
        out_shape=jax.ShapeDtypeStruct((depth, SUB, nm), F32),
        compiler_params=_cp("arbitrary", "arbitrary"),
        name="mod_vectors",
    )(cvec, w_mod, b_mod.reshape(depth, 1, nm))


def _norm_mod_kernel(x_ref, g_ref, sh_ref, sc_ref, o_ref):
    x = x_ref[...]
    y = x * lax.rsqrt(jnp.mean(x * x, axis=-1, keepdims=True) + EPS) * g_ref[...]
    o_ref[...] = (y * (1.0 + sc_ref[...]) + sh_ref[...]).astype(o_ref.dtype)


def _seg_of_block(i, lat_blocks, blocks_per_batch):
    return jnp.where(i < lat_blocks, i // blocks_per_batch, lat_blocks // blocks_per_batch)


def _norm_mod(x, g, modt, l, k_shift, k_scale, lat_blocks, bpb):
    rows, d = x.shape

    def mod_spec(k):
        return pl.BlockSpec((None, None, 1, d),
                            lambda i: (l, _seg_of_block(i, lat_blocks, bpb) * N_MOD + k, 0, 0))

    return pl.pallas_call(
        _norm_mod_kernel,
        grid=(rows // SEG,),
        in_specs=[pl.BlockSpec((SEG, d), lambda i: (i, 0)),
                  pl.BlockSpec((None, 1, d), lambda i: (l, 0, 0)),
                  mod_spec(k_shift), mod_spec(k_scale)],
        out_specs=pl.BlockSpec((SEG, d), lambda i: (i, 0)),
        out_shape=jax.ShapeDtypeStruct((rows, d), BF16),
        compiler_params=_cp("arbitrary"),
        name="norm_modulate",
    )(x, g, modt, modt)


def _mm_kernel(a_ref, w_ref, b_ref, o_ref, *, epilogue):
    acc = jnp.dot(a_ref[...], w_ref[...].astype(BF16), preferred_element_type=F32) + b_ref[...]
    o_ref[...] = epilogue(acc).astype(o_ref.dtype)


def _mm_glu_kernel(a_ref, wp_ref, wq_ref, bp_ref, bq_ref, o_ref):
    a = a_ref[...]
    p = jnp.dot(a, wp_ref[...].astype(BF16), preferred_element_type=F32) + bp_ref[...]
    q = jnp.dot(a, wq_ref[...].astype(BF16), preferred_element_type=F32) + bq_ref[...]
    o_ref[...] = (p * _sigmoid(q)).astype(o_ref.dtype)


def _mm_nobias_kernel(a_ref, w_ref, o_ref):
    o_ref[...] = jnp.dot(a_ref[...], w_ref[...].astype(BF16),
                         preferred_element_type=F32).astype(o_ref.dtype)


def _mm(a, w, b, l, n0, n_len, epilogue, out_dtype, rows):
    k = a.shape[1]
    tm = _pick_tm(rows, MM_TM_MAX)
    tn = 2 * MM_TN if (n_len % (2 * MM_TN) == 0 and n0 % (2 * MM_TN) == 0) else MM_TN
    j0 = n0 // tn
    return pl.pallas_call(
        functools.partial(_mm_kernel, epilogue=epilogue),
        grid=(rows // tm, n_len // tn),
        in_specs=[pl.BlockSpec((tm, k), lambda i, j: (i, 0)),
                  pl.BlockSpec((None, k, tn), lambda i, j: (l, 0, j0 + j)),
                  pl.BlockSpec((None, 1, tn), lambda i, j: (l, 0, j0 + j))],
        out_specs=pl.BlockSpec((tm, tn), lambda i, j: (i, j)),
        out_shape=jax.ShapeDtypeStruct((rows, n_len), out_dtype),
        compiler_params=_cp("arbitrary", "arbitrary"),
        name="matmul_bias_act",
    )(a, w, b)


def _mm_glu(a, w, b, l, n0, n_half, rows):
    k = a.shape[1]
    tm = _pick_tm(rows, MM_TM_MAX)
    tn = MM_TN
    jp = n0 // tn
    jq = (n0 + n_half) // tn
    return pl.pallas_call(
        _mm_glu_kernel,
        grid=(rows // tm, n_half // tn),
        in_specs=[pl.BlockSpec((tm, k), lambda i, j: (i, 0)),
                  pl.BlockSpec((None, k, tn), lambda i, j: (l, 0, jp + j)),
                  pl.BlockSpec((None, k, tn), lambda i, j: (l, 0, jq + j)),
                  pl.BlockSpec((None, 1, tn), lambda i, j: (l, 0, jp + j)),
                  pl.BlockSpec((None, 1, tn), lambda i, j: (l, 0, jq + j))],
        out_specs=pl.BlockSpec((tm, tn), lambda i, j: (i, j)),
        out_shape=jax.ShapeDtypeStruct((rows, n_half), BF16),
        compiler_params=_cp("arbitrary", "arbitrary"),
        name="matmul_glu",
    )(a, w, w, b, b)


def _mm_out(a, w, l):
    rows, k = a.shape
    n = w.shape[-1]
    assert rows % 16 == 0
    tm = _pick_tm(rows, MM_TM_MAX)
    tn = 2 * MM_TN if n % (2 * MM_TN) == 0 else MM_TN
    return pl.pallas_call(
        _mm_nobias_kernel,
        grid=(rows // tm, n // tn),
        in_specs=[pl.BlockSpec((tm, k), lambda i, j: (i, 0)),
                  pl.BlockSpec((None, k, tn), lambda i, j: (l, 0, j))],
        out_specs=pl.BlockSpec((tm, tn), lambda i, j: (i, j)),
        out_shape=jax.ShapeDtypeStruct((rows, n), F32),
        compiler_params=_cp("arbitrary", "arbitrary"),
        name="matmul_out",
    )(a, w)


def _merge_kernel(ya_ref, yb_ref, yc_ref, wa_ref, wb_ref, wc_ref, ga_ref, gb_ref, gc_ref, o_ref):
    def branch(y_ref, w_ref, g_ref):
        return g_ref[...].astype(F32) * jnp.dot(y_ref[...], w_ref[...].astype(BF16),
                                                preferred_element_type=F32)

    acc = branch(ya_ref, wa_ref, ga_ref) + branch(yb_ref, wb_ref, gb_ref) + branch(yc_ref, wc_ref, gc_ref)
    o_ref[...] = acc.astype(o_ref.dtype)


def _merge(ya, yb, yc, wpa, wpb, wpc, gates, l):
    rows, k = yb.shape
    d = wpa.shape[-1]
    tm = _pick_tm(rows, MM_TM_MAX)
    tn = MM_TN
    nd = d // tn
    a_spec = pl.BlockSpec((tm, k), lambda i, j: (i, 0))
    w_spec = pl.BlockSpec((None, k, tn), lambda i, j: (l, 0, j))

    def g_spec(br):
        return pl.BlockSpec((tm, tn), lambda i, j: (i, br * nd + j))

    return pl.pallas_call(
        _merge_kernel,
        grid=(rows // tm, nd),
        in_specs=[a_spec, a_spec, a_spec, w_spec, w_spec, w_spec, g_spec(0), g_spec(1), g_spec(2)],
        out_specs=pl.BlockSpec((tm, tn), lambda i, j: (i, j)),
        out_shape=jax.ShapeDtypeStruct((rows, d), BF16),
        compiler_params=_cp("arbitrary", "arbitrary"),
        name="merge_branches",
    )(ya, yb, yc, wpa, wpb, wpc, gates, gates, gates)


def _rglru_coeffs(x_ref, prev_ref, next_ref, cw_ref, cb_ref, wr_ref, br_ref, wi_ref, bi_ref, lam_ref,
                  a_scr, b_scr, r_scr, i_scr, first, last):
    tc, w = a_scr.shape
    hd = w // H_A
    x = x_ref[...].astype(F32)
    row = lax.broadcasted_iota(jnp.int32, (tc, w), 0)
    zero_row = jnp.zeros((1, w), F32)
    prev = prev_ref[...].astype(F32)
    nxt = next_ref[...].astype(F32)
    p_last = jnp.where(first, zero_row, prev[15:16, :])
    n0 = jnp.where(last, zero_row, nxt[0:1, :])
    n1 = jnp.where(last, zero_row, nxt[1:2, :])
    x_m1 = jnp.where(row == 0, p_last, pltpu.roll(x, 1, axis=0))
    x_p1 = jnp.where(row == tc - 1, n0, pltpu.roll(x, tc - 1, axis=0))
    x_p2 = jnp.where(row == tc - 2, n0, jnp.where(row == tc - 1, n1, pltpu.roll(x, tc - 2, axis=0)))
    xc = (cw_ref[0:1, :] * x_m1 + cw_ref[1:2, :] * x + cw_ref[2:3, :] * x_p1
          + cw_ref[3:4, :] * x_p2 + cb_ref[...])
    xb = xc.astype(BF16)
    for h in range(H_A):
        sl = slice(h * hd, (h + 1) * hd)
        xh = xb[:, sl]
        r_scr[:, sl] = jnp.dot(xh, wr_ref[h].astype(BF16), preferred_element_type=F32)
        i_scr[:, sl] = jnp.dot(xh, wi_ref[h].astype(BF16), preferred_element_type=F32)
    r = _sigmoid(r_scr[...] + br_ref[...])
    ig = _sigmoid(i_scr[...] + bi_ref[...])
    nl = -lam_ref[...]
    softplus = jnp.maximum(nl, 0.0) + jnp.log1p(jnp.exp(-jnp.abs(nl)))
    log_a = (-LRU_C) * r * softplus
    a_scr[...] = jnp.exp(log_a)
    t = jnp.tanh(log_a)
    one_m = (-2.0 * t) / (1.0 - t)
    b_scr[...] = jnp.sqrt(one_m) * (ig * xc)


def _block_scan(a, b, reverse):
    row = lax.broadcasted_iota(jnp.int32, a.shape, 0)
    for d in (1, 2, 4):
        if reverse:
            keep = row < SUB - d
            shift = SUB - d
        else:
            keep = row >= d
            shift = d
        a_sh = jnp.where(keep, pltpu.roll(a, shift, axis=0), 1.0)
        b_sh = jnp.where(keep, pltpu.roll(b, shift, axis=0), 0.0)
        b = a * b_sh + b
        a = a * a_sh
    return a, b


def _rglru_scan(a_scr, b_scr, h_scr, emit, reverse):
    tc, w = a_scr.shape
    nblk = tc // SUB

    def body(k, h):
        kk = (nblk - 1 - k) if reverse else k
        rows = pl.ds(pl.multiple_of(kk * SUB, SUB), SUB)
        ap, bp = _block_scan(a_scr[rows, :], b_scr[rows, :], reverse)
        out = ap * h + bp
        emit(rows, out)
        edge = out[0:1, :] if reverse else out[SUB - 1:SUB, :]
        return jnp.broadcast_to(edge, (SUB, w))

    h_scr[...] = lax.fori_loop(0, nblk, body, h_scr[...])


def _rglru_fwd_kernel(x_ref, prev_ref, next_ref, cw_ref, cb_ref, wr_ref, br_ref, wi_ref, bi_ref, lam_ref,
                      hf_ref, a_scr, b_scr, r_scr, i_scr, h_scr, *, bpb):
    j = pl.program_id(1)

    @pl.when(j == 0)
    def _():
        h_scr[...] = jnp.zeros_like(h_scr)

    first = jnp.logical_or(j == 0, j == 1)
    last = jnp.logical_or(j == 0, j == bpb)
    _rglru_coeffs(x_ref, prev_ref, next_ref, cw_ref, cb_ref, wr_ref, br_ref, wi_ref, bi_ref, lam_ref,
                  a_scr, b_scr, r_scr, i_scr, first, last)

    def emit(rows, out):
        hf_ref[rows, :] = out

    _rglru_scan(a_scr, b_scr, h_scr, emit, reverse=False)


def _rglru_bwd_kernel(x_ref, prev_ref, next_ref, cw_ref, cb_ref, wr_ref, br_ref, wi_ref, bi_ref, lam_ref,
                      hf_ref, ga_ref, ya_ref, a_scr, b_scr, r_scr, i_scr, h_scr, *, bpb):
    j = pl.program_id(1)

    @pl.when(j == 0)
    def _():
        h_scr[...] = jnp.zeros_like(h_scr)

    first = jnp.logical_or(j == 0, j == bpb)
    last = jnp.logical_or(j == 0, j == 1)
    _rglru_coeffs(x_ref, prev_ref, next_ref, cw_ref, cb_ref, wr_ref, br_ref, wi_ref, bi_ref, lam_ref,
                  a_scr, b_scr, r_scr, i_scr, first, last)

    def emit(rows, out):
        ya_ref[rows, :] = ((hf_ref[rows, :] + out) * ga_ref[rows, :].astype(F32)).astype(ya_ref.dtype)

    _rglru_scan(a_scr, b_scr, h_scr, emit, reverse=True)


def _rglru(xa, ga, p, l, nbatch, bpb):
    rows, w = xa.shape
    hd = w // H_A
    lat_blocks = nbatch * bpb
    ga_blocks = ga.shape[0] // SEG
    nhalo = rows // 16
    per_blk = SEG // 16

    def common_specs(direction, blk):
        def prev_map(b, j):
            return (jnp.maximum(blk(b, j) * per_blk - 1, 0), 0)

        def next_map(b, j):
            return (jnp.minimum((blk(b, j) + 1) * per_blk, nhalo - 1), 0)

        vec = pl.BlockSpec((None, None, 1, w), lambda b, j: (l, direction, 0, 0))
        gw = pl.BlockSpec((None, None, H_A, hd, hd), lambda b, j: (l, direction, 0, 0, 0))
        return [pl.BlockSpec((SEG, w), lambda b, j: (blk(b, j), 0)),
                pl.BlockSpec((16, w), prev_map),
                pl.BlockSpec((16, w), next_map),
                pl.BlockSpec((None, CONV_A, w), lambda b, j: (l, 0, 0)),
                pl.BlockSpec((None, 1, w), lambda b, j: (l, 0, 0)),
                gw, vec, gw, vec, vec]

    def blk_f(b, j):
        return jnp.where(j == 0, lat_blocks + b, b * bpb + j - 1)

    def blk_b(b, j):
        return jnp.where(j == 0, lat_blocks + b, b * bpb + bpb - j)

    scratch = [pltpu.VMEM((SEG, w), F32)] * 4 + [pltpu.VMEM((SUB, w), F32)]
    weights = (p["rg_conv_w"], p["rg_conv_b"], p["rg_w_r"], p["rg_b_r"], p["rg_w_i"], p["rg_b_i"], p["rg_lambda"])
    hf = pl.pallas_call(
        functools.partial(_rglru_fwd_kernel, bpb=bpb),
        grid=(nbatch, bpb + 1),
        in_specs=common_specs(0, blk_f),
        out_specs=pl.BlockSpec((SEG, w), lambda b, j: (blk_f(b, j), 0)),
        out_shape=jax.ShapeDtypeStruct((rows, w), F32),
        scratch_shapes=scratch,
        compiler_params=_cp("arbitrary", "arbitrary"),
        name="rglru_forward",
    )(xa, xa, xa, *weights)
    return pl.pallas_call(
        functools.partial(_rglru_bwd_kernel, bpb=bpb),
        grid=(nbatch, bpb + 1),
        in_specs=common_specs(1, blk_b) + [pl.BlockSpec((SEG, w), lambda b, j: (blk_b(b, j), 0)),
                                           pl.BlockSpec((SEG, w), lambda b, j: (jnp.minimum(blk_b(b, j), ga_blocks - 1), 0))],
        out_specs=pl.BlockSpec((SEG, w), lambda b, j: (blk_b(b, j), 0)),
        out_shape=jax.ShapeDtypeStruct((rows, w), BF16),
        scratch_shapes=scratch,
        compiler_params=_cp("arbitrary", "arbitrary"),
        name="rglru_backward",
    )(xa, xa, xa, *weights, hf, ga)


def _layernorm(x, g, b):
    mu = jnp.mean(x, axis=-1, keepdims=True)
    xc = x - mu
    var = jnp.mean(xc * xc, axis=-1, keepdims=True)
    return xc * lax.rsqrt(var + EPS) * g + b


def _sgu_kernel(u_ref, v_ref, g_ref, b_ref, ws_ref, bs_ref, o_ref):
    tc, w = u_ref.shape
    gw = w // G_B
    v = _layernorm(v_ref[...].astype(F32), g_ref[...], b_ref[...]).astype(BF16)
    for n in range(tc // CHUNK):
        rs = slice(n * CHUNK, (n + 1) * CHUNK)
        for g in range(G_B):
            cs = slice(g * gw, (g + 1) * gw)
            s = jnp.dot(ws_ref[g].astype(BF16), v[rs, cs], preferred_element_type=F32)
            o_ref[rs, cs] = (u_ref[rs, cs].astype(F32) * (s + bs_ref[:, cs])).astype(o_ref.dtype)


def _sgu(uv, p, l):
    rows, w2 = uv.shape
    w = w2 // 2
    depth = p["sg_b"].shape[0]
    bs = jnp.repeat(jnp.swapaxes(p["sg_b"], 1, 2), w // G_B, axis=2)
    vec = pl.BlockSpec((None, 1, w), lambda i: (l, 0, 0))
    return pl.pallas_call(
        _sgu_kernel,
        grid=(rows // SEG,),
        in_specs=[pl.BlockSpec((SEG, w), lambda i: (i, 0)),
                  pl.BlockSpec((SEG, w), lambda i: (i, 1)),
                  vec, vec,
                  pl.BlockSpec((None, G_B, CHUNK, CHUNK), lambda i: (l, 0, 0, 0)),
                  pl.BlockSpec((None, CHUNK, w), lambda i: (l, 0, 0))],
        out_specs=pl.BlockSpec((SEG, w), lambda i: (i, 0)),
        out_shape=jax.ShapeDtypeStruct((rows, w), BF16),
        compiler_params=_cp("arbitrary"),
        name="spatial_gating",
    )(uv, uv, p["sg_ln_g"].reshape(depth, 1, w), p["sg_ln_b"].reshape(depth, 1, w), p["sg_w"], bs)


def _conv_lat_kernel(x_ref, cw_ref, cb_ref, g_ref, b_ref, o_ref, xp, ys):
    nr, wc, c = x_ref.shape
    half = CONV_C // 2
    zeros = jnp.zeros((half, wc, c), F32)
    xp[0:half, :, 0:c] = zeros
    xp[half + nr:half + nr + half, :, 0:c] = zeros
    xp[half:half + nr, :, 0:c] = x_ref[...].astype(F32)

    for g in range(c // LANES):
        ls = slice(g * LANES, (g + 1) * LANES)
        wv = [jnp.broadcast_to(cw_ref[k:k + 1, ls], (SUB, LANES)) for k in range(CONV_C)]

        def taps(r, carry, ls=ls, wv=wv):
            for hs in range(wc // SUB):
                ss = slice(hs * SUB, (hs + 1) * SUB)
                ys[r, ss, ls] = _tree_sum([wv[k] * xp[r + k, ss, ls] for k in range(CONV_C)])
            return carry

        lax.fori_loop(0, nr, taps, 0, unroll=2)

    def finish(r, carry):
        y = _layernorm(ys[r] + cb_ref[...], g_ref[...], b_ref[...])
        o_ref[r] = _silu(y).astype(o_ref.dtype)
        return carry

    lax.fori_loop(0, nr, finish, 0, unroll=2)


def _conv_ctx_kernel(x_ref, cw_ref, cb_ref, g_ref, b_ref, o_ref, xp):
    n, c = x_ref.shape
    half = CONV_C // 2
    pad = 16
    xp[0:pad, :] = jnp.zeros((pad, c), F32)
    xp[pad + n:pad + n + pad, :] = jnp.zeros((pad, c), F32)
    xp[pad:pad + n, :] = x_ref[...].astype(F32)
    acc = cw_ref[0:1, :] * xp[pad - half:pad - half + n, :]
    for k in range(1, CONV_C):
        off = pad - half + k
        acc = acc + cw_ref[k:k + 1, :] * xp[off:off + n, :]
    y = _layernorm(acc + cb_ref[...], g_ref[...], b_ref[...])
    o_ref[...] = _silu(y).astype(o_ref.dtype)


def _conformer(glu, p, l, nbatch, seq, ctx_len, with_ctx):
    rows, c = glu.shape
    depth = p["cv_b"].shape[0]
    nrow = seq // GRID_W
    wt = 16
    cw = p["cv_w"]
    vecs = [p["cv_b"].reshape(depth, 1, c), p["cv_ln_g"].reshape(depth, 1, c), p["cv_ln_b"].reshape(depth, 1, c)]
    g3 = glu.reshape(rows // GRID_W, GRID_W, c)
    half = CONV_C // 2

    lat = pl.pallas_call(
        _conv_lat_kernel,
        grid=(nbatch, GRID_W // wt),
        in_specs=[pl.BlockSpec((nrow, wt, c), lambda b, j: (b, j, 0)),
                  pl.BlockSpec((None, CONV_C, c), lambda b, j: (l, 0, 0))]
                 + [pl.BlockSpec((None, 1, c), lambda b, j: (l, 0, 0))] * 3,
        out_specs=pl.BlockSpec((nrow, wt, c), lambda b, j: (b, j, 0)),
        out_shape=jax.ShapeDtypeStruct((nbatch * nrow, GRID_W, c), BF16),
        scratch_shapes=[pltpu.VMEM((nrow + 2 * half, wt, c + LANES), F32), pltpu.VMEM((nrow, wt, c), F32)],
        compiler_params=_cp("arbitrary", "arbitrary"),
        name="conformer_conv_latent",
    )(g3, cw, *vecs)
    lat = lat.reshape(nbatch * seq, c)
    if not with_ctx:
        return lat
    cblk0 = nbatch * seq // ctx_len
    cx = pl.pallas_call(
        _conv_ctx_kernel,
        grid=(nbatch,),
        in_specs=[pl.BlockSpec((ctx_len, c), lambda b: (cblk0 + b, 0)),
                  pl.BlockSpec((None, CONV_C, c), lambda b: (l, 0, 0))]
                 + [pl.BlockSpec((None, 1, c), lambda b: (l, 0, 0))] * 3,
        out_specs=pl.BlockSpec((ctx_len, c), lambda b: (b, 0)),
        out_shape=jax.ShapeDtypeStruct((nbatch * ctx_len, c), BF16),
        scratch_shapes=[pltpu.VMEM((ctx_len + 32, c), F32)],
        compiler_params=_cp("arbitrary"),
        name="conformer_conv_context",
    )(glu, cw, *vecs)
    return jnp.concatenate([lat, cx], axis=0)


def _ffn_pre_kernel(x_ref, o_ref, gm_ref, g_ref, sh_ref, sc_ref, wr_ref, xo_ref, hx_ref, aff_ref):
    d = x_ref.shape[1]
    dh = d // 2
    x = x_ref[...] + gm_ref[...] * o_ref[...]
    xo_ref[...] = x
    y = x * lax.rsqrt(jnp.mean(x * x, axis=-1, keepdims=True) + EPS) * g_ref[...]
    h = y * (1.0 + sc_ref[...]) + sh_ref[...]
    w = wr_ref[...]
    h_hi = h.astype(BF16)
    h_lo = (h - h_hi.astype(F32)).astype(BF16)
    w_hi = w.astype(BF16)
    w_lo = (w - w_hi.astype(F32)).astype(BF16)
    logits = (jnp.dot(h_hi, w_hi, preferred_element_type=F32)
              + (jnp.dot(h_lo, w_hi, preferred_element_type=F32)
                 + jnp.dot(h_hi, w_lo, preferred_element_type=F32)))
    lane = lax.broadcasted_iota(jnp.int32, logits.shape, 1)
    valid = lane < N_EXP
    m = jnp.max(jnp.where(valid, logits, -jnp.inf), axis=-1, keepdims=True)
    ex = jnp.where(valid, jnp.exp(logits - m), 0.0)
    aff = ex / jnp.sum(ex, axis=-1, keepdims=True)
    aff_ref[...] = aff
    lo = pltpu.bitcast(h[:, :dh].astype(BF16).astype(F32), jnp.int32)
    hi = pltpu.bitcast(h[:, dh:].astype(BF16).astype(F32), jnp.int32)
    hx_ref[:, :dh] = jnp.bitwise_or(jnp.bitwise_and(hi, jnp.int32(-65536)),
                                    lax.shift_right_logical(lo, jnp.int32(16)))
    hx_ref[:, dh:] = pltpu.bitcast(aff, jnp.int32)


def _ffn_pre(x, o, g, modt, wr_pad, l, lat_blocks, bpb):
    rows = o.shape[0]
    d = x.shape[1]
    blk = pl.BlockSpec((SEG, d), lambda i: (i, 0))

    def mod_spec(k):
        return pl.BlockSpec((None, None, 1, d),
                            lambda i: (l, _seg_of_block(i, lat_blocks, bpb) * N_MOD + k, 0, 0))

    return pl.pallas_call(
        _ffn_pre_kernel,
        grid=(rows // SEG,),
        in_specs=[blk, blk, mod_spec(2),
                  pl.BlockSpec((None, 1, d), lambda i: (l, 0, 0)),
                  mod_spec(3), mod_spec(4),
                  pl.BlockSpec((None, d, LANES), lambda i: (l, 0, 0))],
        out_specs=[blk,
                   pl.BlockSpec((SEG, d // 2 + LANES), lambda i: (i, 0)),
                   pl.BlockSpec((SEG, LANES), lambda i: (i, 0))],
        out_shape=[jax.ShapeDtypeStruct(x.shape, F32),
                   jax.ShapeDtypeStruct((rows, d // 2 + LANES), jnp.int32),
                   jax.ShapeDtypeStruct((rows, LANES), F32)],
        input_output_aliases={0: 0},
        compiler_params=_cp("arbitrary"),
        name="ffn_pre_router",
    )(x, o, modt, g, modt, modt, wr_pad)


def _prefix_mats():
    r = lax.broadcasted_iota(jnp.int32, (LANES, LANES), 0)
    c = lax.broadcasted_iota(jnp.int32, (LANES, LANES), 1)
    incl = jnp.where(r <= c, 1.0, 0.0).astype(BF16)
    strict = jnp.where(c < r, 1.0, 0.0).astype(BF16)
    return incl, strict


def _topk_kernel(a_ref, o_ref, thr_scr, *, cap):
    nexp, nc, _ = a_ref.shape
    bits_all = pltpu.bitcast(a_ref[...], jnp.int32)
    thr_all = jnp.zeros((nexp, 1, 1), jnp.int32)
    for bit in range(30, -1, -1):
        cand = thr_all | jnp.int32(1 << bit)
        ge = jnp.where(bits_all >= cand, 1.0, 0.0)
        cnt_all = jnp.sum(jnp.sum(ge, axis=2, keepdims=True), axis=1, keepdims=True)
        thr_all = jnp.where(cnt_all >= float(cap), cand, thr_all)
    thr_scr[...] = jnp.broadcast_to(thr_all, thr_scr.shape)
    incl, strict = _prefix_mats()
    ones_sq = jnp.ones((LANES, LANES), BF16)
    ones8 = jnp.ones((SUB, LANES), BF16)
    lane_c = lax.broadcasted_iota(jnp.int32, (cap, LANES), 1)
    s_col = lax.broadcasted_iota(jnp.int32, (cap, LANES), 0).astype(F32)
    capf = jnp.float32(cap)
    o_ref[...] = jnp.zeros_like(o_ref)

    def total(x):
        return jnp.sum(jnp.sum(x, axis=1, keepdims=True), axis=0, keepdims=True)

    def pad_rows(x):
        if nc == LANES:
            return x
        return jnp.concatenate([x, jnp.zeros((LANES - nc, LANES), x.dtype)], axis=0)

    def prefix(mask_f):
        mp = pad_rows(mask_f).astype(BF16)
        within = jnp.dot(mp, incl, preferred_element_type=F32)
        tot = jnp.dot(mp, ones_sq, preferred_element_type=F32).astype(BF16)
        excl = jnp.dot(strict, tot, preferred_element_type=F32)
        return mp, within, within + excl

    def body(e, carry):
        bits = pltpu.bitcast(a_ref[e], jnp.int32)
        thr = thr_scr[e][0:1, :]
        gt = bits > thr
        eq = bits == thr
        need = capf - total(jnp.where(gt, 1.0, 0.0))
        _, _, pe = prefix(jnp.where(eq, 1.0, 0.0))
        take = jnp.logical_or(gt, jnp.logical_and(eq, pe[:nc] <= need))
        mp, within, _ = prefix(jnp.where(take, 1.0, 0.0))
        s_row = lax.dot_general(ones8, mp, (((1,), (1,)), ((), ())), preferred_element_type=F32)
        pend_row = jnp.dot(s_row.astype(BF16), incl, preferred_element_type=F32)
        pend_b = jnp.broadcast_to(pend_row[0:1, :], (cap, LANES))
        s_b = jnp.broadcast_to(s_row[0:1, :], (cap, LANES))
        before = jnp.logical_and(pend_b <= s_col, lane_c < nc)
        c_s = jnp.sum(jnp.where(before, 1.0, 0.0), axis=1, keepdims=True)
        p_excl = jnp.sum(jnp.where(before, s_b, 0.0), axis=1, keepdims=True)
        sel = jnp.where(lane_c.astype(F32) == c_s, 1.0, 0.0).astype(BF16)
        w_row = jnp.dot(sel, within.astype(BF16), preferred_element_type=F32)
        cnt = jnp.sum(jnp.where(w_row <= s_col - p_excl, 1.0, 0.0), axis=1, keepdims=True)
        idx = (c_s * float(LANES) + cnt).astype(jnp.int32)
        o_ref[...] = jnp.where(lane_c == e, idx, o_ref[...])
        return carry

    lax.fori_loop(0, nexp, body, 0)


def _topk(aff_t, cap):
    nsets, nexp, nc, _ = aff_t.shape
    return pl.pallas_call(
        functools.partial(_topk_kernel, cap=cap),
        grid=(nsets,),
        in_specs=[pl.BlockSpec((None, nexp, nc, LANES), lambda s: (s, 0, 0, 0))],
        out_specs=pl.BlockSpec((None, cap, LANES), lambda s: (s, 0, 0)),
        out_shape=jax.ShapeDtypeStruct((nsets, cap, LANES), jnp.int32),
        scratch_shapes=[pltpu.VMEM((nexp, SUB, LANES), jnp.int32)],
        compiler_params=_cp("arbitrary"),
        name="expert_choice_topk",
    )(aff_t)


def _expert_kernel(idx_ref, hx_hbm, wg_ref, wu_ref, wd_ref, *refs, nf, segs):
    nseg = len(segs)
    gm_refs = refs[:nseg]
    o_ref, xg, xlo, xhi, hid, gate, sem = refs[nseg:]
    e = pl.program_id(0)
    s = pl.program_id(1)
    nr = xg.shape[0]
    dh = xlo.shape[1]
    tf = wg_ref.shape[1]

    def row_copy(r, row):
        return pltpu.make_async_copy(hx_hbm.at[pl.ds(row, 1)], xg.at[pl.ds(r, 1)], sem)

    @pl.when(s == 0)
    def _():
        def issue(r, c):
            row_copy(r, idx_ref[e * nr + r]).start()
            return c

        lax.fori_loop(0, nr, issue, 0, unroll=DMA_UNROLL)

        def drain(r, c):
            row_copy(r, 0).wait()
            return c

        lax.fori_loop(0, nr, drain, 0, unroll=DMA_UNROLL)

        def unpack(k, c):
            rows = pl.ds(pl.multiple_of(k * ROW_CHUNK, ROW_CHUNK), ROW_CHUNK)
            u = xg[rows, :dh]
            xlo[rows, :] = pltpu.bitcast(lax.shift_left(u, jnp.int32(16)), F32).astype(BF16)
            xhi[rows, :] = pltpu.bitcast(jnp.bitwise_and(u, jnp.int32(-65536)), F32).astype(BF16)
            aff = pltpu.bitcast(xg[rows, dh:], F32)
            lane = lax.broadcasted_iota(jnp.int32, aff.shape, 1)
            g = jnp.sum(jnp.where(lane == e, aff, 0.0), axis=1, keepdims=True)
            gate[rows, :] = jnp.broadcast_to(g, aff.shape)
            return c

        lax.fori_loop(0, nr // ROW_CHUNK, unpack, 0, unroll=2)

    @pl.when(s < nf)
    def _():
        def proj(w_ref):
            return (jnp.dot(xlo[...], w_ref[:dh, :].astype(BF16), preferred_element_type=F32)
                    + jnp.dot(xhi[...], w_ref[dh:, :].astype(BF16), preferred_element_type=F32))

        hid[s] = (_silu(proj(wg_ref)) * proj(wu_ref)).astype(BF16)

    @pl.when(s >= nf)
    def _():
        acc = jnp.dot(hid[0], wd_ref[0:tf, :].astype(BF16), preferred_element_type=F32)
        for f in range(1, nf):
            acc = acc + jnp.dot(hid[f], wd_ref[f * tf:(f + 1) * tf, :].astype(BF16),
                                preferred_element_type=F32)
        for (r0, r1), gm_ref in zip(segs, gm_refs):
            for k in range(o_ref.shape[1] // LANES):
                cs = slice(k * LANES, (k + 1) * LANES)
                o_ref[r0:r1, cs] = (acc[r0:r1, cs] * gate[r0:r1, :] * gm_ref[:, cs]).astype(o_ref.dtype)


def _experts(idx_flat, hx, w_g, w_u, w_d, modt, l, nr, segs, seg_ids):
    d, dexp = w_g.shape[-2:]
    tf = 256 if dexp % 256 == 0 else dexp
    td = 512 if d % 512 == 0 else d
    nf = dexp // tf
    nd = d // td
    wcols = hx.shape[1]

    def gm_spec(seg):
        return pl.BlockSpec((None, None, 1, td),
                            lambda e, s, idx: (l, seg * N_MOD + 5, 0, jnp.maximum(s - nf, 0)))

    grid_spec = pltpu.PrefetchScalarGridSpec(
        num_scalar_prefetch=1,
        grid=(N_EXP, nf + nd),
        in_specs=[pl.BlockSpec(memory_space=pl.ANY),
                  pl.BlockSpec((None, None, d, tf), lambda e, s, idx: (l, e, 0, jnp.minimum(s, nf - 1))),
                  pl.BlockSpec((None, None, d, tf), lambda e, s, idx: (l, e, 0, jnp.minimum(s, nf - 1))),
                  pl.BlockSpec((None, None, dexp, td), lambda e, s, idx: (l, e, 0, jnp.maximum(s - nf, 0)))]
                 + [gm_spec(sg) for sg in seg_ids],
        out_specs=pl.BlockSpec((None, nr, td), lambda e, s, idx: (e, 0, jnp.maximum(s - nf, 0))),
        scratch_shapes=[pltpu.VMEM((nr, wcols), jnp.int32),
                        pltpu.VMEM((nr, d // 2), BF16),
                        pltpu.VMEM((nr, d // 2), BF16),
                        pltpu.VMEM((nf, nr, tf), BF16),
                        pltpu.VMEM((nr, LANES), F32),
                        pltpu.SemaphoreType.DMA(())],
    )
    return pl.pallas_call(
        functools.partial(_expert_kernel, nf=nf, segs=segs),
        grid_spec=grid_spec,
        out_shape=jax.ShapeDtypeStruct((N_EXP, nr, d), BF16),
        compiler_params=_cp("arbitrary", "arbitrary"),
        name="experts_swiglu",
    )(idx_flat, hx, w_g, w_u, w_d, *([modt] * len(seg_ids)))


def _scatter_kernel(idx_ref, ye_ref, x_in, x_out, buf, rsem, wsem):
    del x_in
    e = pl.program_id(0)
    nr = buf.shape[0]

    def rd(r, row):
        return pltpu.make_async_copy(x_out.at[pl.ds(row, 1)], buf.at[pl.ds(r, 1)], rsem)

    def wr(r, row):
        return pltpu.make_async_copy(buf.at[pl.ds(r, 1)], x_out.at[pl.ds(row, 1)], wsem)

    def each(fn):
        def body(r, carry):
            fn(r)
            return carry

        lax.fori_loop(0, nr, body, 0, unroll=DMA_UNROLL)

    each(lambda r: rd(r, idx_ref[e * nr + r]).start())
    each(lambda r: rd(r, 0).wait())

    def add(k, carry):
        rows = pl.ds(pl.multiple_of(k * ROW_CHUNK, ROW_CHUNK), ROW_CHUNK)
        buf[rows, :] = buf[rows, :] + ye_ref[rows, :].astype(F32)
        return carry

    lax.fori_loop(0, nr // ROW_CHUNK, add, 0)
    each(lambda r: wr(r, idx_ref[e * nr + r]).start())
    each(lambda r: wr(r, 0).wait())


def _scatter_add(idx_flat, ye, x):
    nexp, nr, d = ye.shape
    grid_spec = pltpu.PrefetchScalarGridSpec(
        num_scalar_prefetch=1,
        grid=(nexp,),
        in_specs=[pl.BlockSpec((None, nr, d), lambda e, idx: (e, 0, 0)),
                  pl.BlockSpec(memory_space=pl.ANY)],
        out_specs=pl.BlockSpec(memory_space=pl.ANY),
        scratch_shapes=[pltpu.VMEM((nr, d), F32), pltpu.SemaphoreType.DMA(()), pltpu.SemaphoreType.DMA(())],
    )
    return pl.pallas_call(
        _scatter_kernel,
        grid_spec=grid_spec,
        out_shape=jax.ShapeDtypeStruct(x.shape, x.dtype),
        input_output_aliases={2: 0},
        compiler_params=_cp("arbitrary"),
        name="scatter_add_residual",
    )(idx_flat, ye, x)


def _final_norm_kernel(x_ref, g_ref, o_ref):
    x = x_ref[...]
    o_ref[...] = x * lax.rsqrt(jnp.mean(x * x, axis=-1, keepdims=True) + EPS) * g_ref[...]


def _final_norm(x, g, nrows):
    d = x.shape[1]
    return pl.pallas_call(
        _final_norm_kernel,
        grid=(nrows // SEG,),
        in_specs=[pl.BlockSpec((SEG, d), lambda i: (i, 0)), pl.BlockSpec((1, d), lambda i: (0, 0))],
        out_specs=pl.BlockSpec((SEG, d), lambda i: (i, 0)),
        out_shape=jax.ShapeDtypeStruct((nrows, d), F32),
        compiler_params=_cp("arbitrary"),
        name="final_norm",
    )(x, g.reshape(1, d))


def _route(aff, nbatch, seq, ctx_len, with_ctx):
    lat = nbatch * seq
    cap_l = CAP_FACTOR * seq // N_EXP
    cap_c = CAP_FACTOR * ctx_len // N_EXP
    a_l = aff[:lat, :N_EXP].reshape(nbatch, seq, N_EXP).transpose(0, 2, 1)
    a_l = a_l.reshape(nbatch, N_EXP, seq // LANES, LANES)
    idx_l = _topk(a_l, cap_l)[:, :, :N_EXP]
    idx_l = idx_l + (jnp.arange(nbatch, dtype=jnp.int32) * seq)[:, None, None]
    segs = tuple((b * cap_l, (b + 1) * cap_l) for b in range(nbatch))
    seg_ids = tuple(range(nbatch))
    if not with_ctx:
        idx = idx_l.transpose(2, 0, 1).reshape(N_EXP, nbatch * cap_l)
        return idx.reshape(-1), idx.shape[1], segs, seg_ids
    cpad = SUB * LANES
    a_c = aff[lat:, :N_EXP].reshape(nbatch, ctx_len, N_EXP).transpose(0, 2, 1)
    a_c = jnp.pad(a_c, ((0, 0), (0, 0), (0, cpad - ctx_len)), constant_values=-1.0)
    a_c = a_c.reshape(nbatch, N_EXP, SUB, LANES)
    idx_c = _topk(a_c, cap_c)[:, :, :N_EXP]
    idx_c = idx_c + (lat + jnp.arange(nbatch, dtype=jnp.int32) * ctx_len)[:, None, None]
    idx = jnp.concatenate([idx_l.transpose(2, 0, 1).reshape(N_EXP, nbatch * cap_l),
                           idx_c.transpose(2, 0, 1).reshape(N_EXP, nbatch * cap_c)], axis=1)
    segs = segs + ((nbatch * cap_l, nbatch * (cap_l + cap_c)),)
    seg_ids = seg_ids + (nbatch,)
    return idx.reshape(-1), idx.shape[1], segs, seg_ids


def kernel(x, c, ctx, c_ctx, w_mod, b_mod, g_mix, g_ffn, w_in, b_in, rg_conv_w, rg_conv_b, rg_w_r, rg_b_r, rg_w_i, rg_b_i, rg_lambda, sg_ln_g, sg_ln_b, sg_w, sg_b, cv_w, cv_b, cv_ln_g, cv_ln_b, w_proj_a, w_proj_b, w_proj_c, w_out, w_router, w_e_gate, w_e_up, w_e_down, g_final):
    nbatch, seq, d = x.shape
    ctx_len = ctx.shape[1]
    depth = w_mod.shape[0]
    w_a = rg_conv_w.shape[-1]
    w_b = sg_ln_g.shape[-1]
    w_c = cv_w.shape[-1]
    assert ctx_len == SEG and seq % SEG == 0 and nbatch + 1 <= SUB
    bpb = seq // SEG
    lat_blocks = nbatch * bpb
    lat_rows = nbatch * seq
    all_rows = lat_rows + nbatch * ctx_len
    off_xa = w_a
    off_uv = 2 * w_a
    off_glu = off_uv + 2 * w_b
    off_gate = off_glu + 2 * w_c

    cvec = jnp.concatenate([c, c_ctx[None], jnp.zeros((SUB - nbatch - 1, d), F32)], axis=0)
    modt = _mod_all(cvec, w_mod, b_mod).reshape(depth, SUB * N_MOD, 1, d)
    xall = jnp.concatenate([x.reshape(lat_rows, d), ctx.reshape(nbatch * ctx_len, d)], axis=0)

    b_in3 = b_in.reshape(depth, 1, -1)
    gmix3 = g_mix.reshape(depth, 1, d)
    gffn3 = g_ffn.reshape(depth, 1, d)
    wr_pad = jnp.pad(w_router, ((0, 0), (0, 0), (0, LANES - N_EXP)))
    rg = {"rg_conv_w": rg_conv_w, "rg_conv_b": rg_conv_b.reshape(depth, 1, w_a),
          "rg_w_r": rg_w_r, "rg_b_r": rg_b_r.reshape(depth, 2, 1, w_a),
          "rg_w_i": rg_w_i, "rg_b_i": rg_b_i.reshape(depth, 2, 1, w_a),
          "rg_lambda": rg_lambda.reshape(depth, 2, 1, w_a)}
    sg = {"sg_ln_g": sg_ln_g, "sg_ln_b": sg_ln_b, "sg_w": sg_w, "sg_b": sg_b}
    cv = {"cv_w": cv_w, "cv_b": cv_b, "cv_ln_g": cv_ln_g, "cv_ln_b": cv_ln_b}

    for l in range(depth):
        need_ctx = l < depth - 1
        nrows = all_rows if need_ctx else lat_rows
        h = _norm_mod(xall, gmix3, modt, l, 0, 1, lat_blocks, bpb)
        ga = _mm(h, w_in, b_in3, l, 0, w_a, _gelu, BF16, nrows)
        xa = _mm(h, w_in, b_in3, l, off_xa, w_a, lambda z: z, BF16, all_rows)
        uv = _mm(h, w_in, b_in3, l, off_uv, 2 * w_b, _gelu, BF16, nrows)
        glu = _mm_glu(h, w_in, b_in3, l, off_glu, w_c, nrows)
        gates = _mm(h, w_in, b_in3, l, off_gate, N_BRANCH * d, _sigmoid, BF16, nrows)
        ya = _rglru(xa, ga, rg, l, nbatch, bpb)
        yb = _sgu(uv, sg, l)
        yc = _conformer(glu, cv, l, nbatch, seq, ctx_len, need_ctx)
        y = _merge(ya, yb, yc, w_proj_a, w_proj_b, w_proj_c, gates, l)
        o = _mm_out(y, w_out, l)
        xall, hx, aff = _ffn_pre(xall, o, gffn3, modt, wr_pad, l, lat_blocks, bpb)
        idx_flat, nr, segs, seg_ids = _route(aff, nbatch, seq, ctx_len, need_ctx)
        ye = _experts(idx_flat, hx, w_e_gate, w_e_up, w_e_down, modt, l, nr, segs, seg_ids)
        xall = _scatter_add(idx_flat, ye, xall)
    return _final_norm(xall, g_final, lat_rows).reshape(nbatch, seq, d)
```

```python
import functools

import jax
import jax.numpy as jnp
from jax import lax
from jax.experimental import pallas as pl
from jax.experimental.pallas import tpu as pltpu

GRID_W = 64
N_MOD = 6
H_A = 16
CONV_A = 4
LRU_C = 8.0
G_B = 16
CHUNK = 128
CONV_C = 31
N_BRANCH = 3
N_EXP = 16
CAP_FACTOR = 2
EPS = 1e-6

LANES = 128
SUB = 8
SEG = 256
VMEM_LIMIT = 56 * 1024 * 1024
MM_TN = 256
MM_TM_MAX = 1152
ROW_CHUNK = 32

BF16 = jnp.bfloat16
F32 = jnp.float32


def _cp(*sem):
    return pltpu.CompilerParams(dimension_semantics=sem, vmem_limit_bytes=VMEM_LIMIT)


def _pick_tm(rows, tm_max):
    k = 1
    while rows % k or (rows // k) > tm_max or (rows // k) % 16:
        k += 1
    return rows // k


def _gelu(x):
    return 0.5 * x * (1.0 + jnp.tanh(0.7978845608028654 * (x + 0.044715 * (x * x * x))))


def _sigmoid(x):
    return 1.0 / (1.0 + jnp.exp(-x))


def _silu(x):
    return x * _sigmoid(x)


def _tree_sum(terms):
    while len(terms) > 1:
        terms = [terms[i] + terms[i + 1] for i in range(0, len(terms) - 1, 2)] + (
            [terms[-1]] if len(terms) % 2 else [])
    return terms[0]


def _mod_kernel(a_ref, w_ref, b_ref, o_ref):
    a = _silu(a_ref[...]).astype(BF16)
    o_ref[...] = jnp.dot(a, w_ref[...].astype(BF16), preferred_element_type=F32) + b_ref[...]


def _mod_all(cvec, w_mod, b_mod):
    depth, d, nm = w_mod.shape
    tn = 1024 if nm % 1024 == 0 else nm
    return pl.pallas_call(
        _mod_kernel,
        grid=(depth, nm // tn),
        in_specs=[pl.BlockSpec((SUB, d), lambda l, j: (0, 0)),
                  pl.BlockSpec((None, d, tn), lambda l, j: (l, 0, j)),
                  pl.BlockSpec((None, 1, tn), lambda l, j: (l, 0, j))],
        out_specs=pl.BlockSpec((None, SUB, tn), lambda l, j: (l, 0, j)),
        out_shape=jax.ShapeDtypeStruct((depth, SUB, nm), F32),
        compiler_params=_cp("arbitrary", "arbitrary"),
        name="mod_vectors",
    )(cvec, w_mod, b_mod.reshape(depth, 1, nm))


def _norm_mod_kernel(x_ref, g_ref, sh_ref, sc_ref, o_ref):
    x = x_ref[...]
    y = x * lax.rsqrt(jnp.mean(x * x, axis=-1, keepdims=True) + EPS) * g_ref[...]
    o_ref[...] = (y * (1.0 + sc_ref[...]) + sh_ref[...]).astype(o_ref.dtype)


def _first_norm_mod_kernel(x_ref, c_ref, g_ref, sh_ref, sc_ref, xall_ref, h_ref, *, lat_blocks):
    def emit(src_ref):
        xall_ref[...] = src_ref[...]
        _norm_mod_kernel(src_ref, g_ref, sh_ref, sc_ref, h_ref)

    is_latent = pl.program_id(0) < lat_blocks
    pl.when(is_latent)(lambda: emit(x_ref))
    pl.when(jnp.logical_not(is_latent))(lambda: emit(c_ref))


def _first_norm_mod(x2, c2, g, modt, l, lat_blocks, bpb):
    d = x2.shape[1]
    rows = x2.shape[0] + c2.shape[0]
    cblocks = c2.shape[0] // SEG

    def mod_spec(k):
        return pl.BlockSpec((None, None, 1, d),
                            lambda i: (l, _seg_of_block(i, lat_blocks, bpb) * N_MOD + k, 0, 0))

    blk = pl.BlockSpec((SEG, d), lambda i: (i, 0))
    return pl.pallas_call(
        functools.partial(_first_norm_mod_kernel, lat_blocks=lat_blocks),
        grid=(rows // SEG,),
        in_specs=[pl.BlockSpec((SEG, d), lambda i: (jnp.minimum(i, lat_blocks - 1), 0)),
                  pl.BlockSpec((SEG, d), lambda i: (jnp.clip(i - lat_blocks, 0, cblocks - 1), 0)),
                  pl.BlockSpec((None, 1, d), lambda i: (l, 0, 0)),
                  mod_spec(0), mod_spec(1)],
        out_specs=[blk, blk],
        out_shape=[jax.ShapeDtypeStruct((rows, d), F32), jax.ShapeDtypeStruct((rows, d), BF16)],
        compiler_params=_cp("arbitrary"),
        name="embed_norm_modulate",
    )(x2, c2, g, modt, modt)


def _seg_of_block(i, lat_blocks, blocks_per_batch):
    return jnp.where(i < lat_blocks, i // blocks_per_batch, lat_blocks // blocks_per_batch)


def _norm_mod(x, g, modt, l, k_shift, k_scale, lat_blocks, bpb):
    rows, d = x.shape

    def mod_spec(k):
        return pl.BlockSpec((None, None, 1, d),
                            lambda i: (l, _seg_of_block(i, lat_blocks, bpb) * N_MOD + k, 0, 0))

    return pl.pallas_call(
        _norm_mod_kernel,
        grid=(rows // SEG,),
        in_specs=[pl.BlockSpec((SEG, d), lambda i: (i, 0)),
                  pl.BlockSpec((None, 1, d), lambda i: (l, 0, 0)),
                  mod_spec(k_shift), mod_spec(k_scale)],
        out_specs=pl.BlockSpec((SEG, d), lambda i: (i, 0)),
        out_shape=jax.ShapeDtypeStruct((rows, d), BF16),
        compiler_params=_cp("arbitrary"),
        name="norm_modulate",
    )(x, g, modt, modt)


def _mm_kernel(a_ref, w_ref, b_ref, o_ref, *, epilogue):
    acc = jnp.dot(a_ref[...], w_ref[...].astype(BF16), preferred_element_type=F32) + b_ref[...]
    o_ref[...] = epilogue(acc).astype(o_ref.dtype)


def _mm_glu_kernel(a_ref, wp_ref, wq_ref, bp_ref, bq_ref, o_ref):
    a = a_ref[...]
    p = jnp.dot(a, wp_ref[...].astype(BF16), preferred_element_type=F32) + bp_ref[...]
    q = jnp.dot(a, wq_ref[...].astype(BF16), preferred_element_type=F32) + bq_ref[...]
    o_ref[...] = (p * _sigmoid(q)).astype(o_ref.dtype)


def _mm(a, w, b, l, n0, n_len, epilogue, out_dtype, rows):
    k = a.shape[1]
    tm = _pick_tm(rows, MM_TM_MAX)
    tn = 2 * MM_TN if (n_len % (2 * MM_TN) == 0 and n0 % (2 * MM_TN) == 0) else MM_TN
    j0 = n0 // tn
    return pl.pallas_call(
        functools.partial(_mm_kernel, epilogue=epilogue),
        grid=(rows // tm, n_len // tn),
        in_specs=[pl.BlockSpec((tm, k), lambda i, j: (i, 0)),
                  pl.BlockSpec((None, k, tn), lambda i, j: (l, 0, j0 + j)),
                  pl.BlockSpec((None, 1, tn), lambda i, j: (l, 0, j0 + j))],
        out_specs=pl.BlockSpec((tm, tn), lambda i, j: (i, j)),
        out_shape=jax.ShapeDtypeStruct((rows, n_len), out_dtype),
        compiler_params=_cp("arbitrary", "arbitrary"),
        name="matmul_bias_act",
    )(a, w, b)


def _mm_glu(a, w, b, l, n0, n_half, rows):
    k = a.shape[1]
    tm = _pick_tm(rows, MM_TM_MAX)
    tn = MM_TN
    jp = n0 // tn
    jq = (n0 + n_half) // tn
    return pl.pallas_call(
        _mm_glu_kernel,
        grid=(rows // tm, n_half // tn),
        in_specs=[pl.BlockSpec((tm, k), lambda i, j: (i, 0)),
                  pl.BlockSpec((None, k, tn), lambda i, j: (l, 0, jp + j)),
                  pl.BlockSpec((None, k, tn), lambda i, j: (l, 0, jq + j)),
                  pl.BlockSpec((None, 1, tn), lambda i, j: (l, 0, jp + j)),
                  pl.BlockSpec((None, 1, tn), lambda i, j: (l, 0, jq + j))],
        out_specs=pl.BlockSpec((tm, tn), lambda i, j: (i, j)),
        out_shape=jax.ShapeDtypeStruct((rows, n_half), BF16),
        compiler_params=_cp("arbitrary", "arbitrary"),
        name="matmul_glu",
    )(a, w, w, b, b)


def _row_segment(r, seq, lat_rows):
    return jnp.where(r < lat_rows, r // seq, lat_rows // seq)


def _mm_out_kernel(a_ref, w_ref, x_ref, ma_ref, mb_ref, o_ref, *, seq, lat_rows):
    tm = x_ref.shape[0]
    acc = jnp.dot(a_ref[...], w_ref[...].astype(BF16), preferred_element_type=F32)
    r0 = pl.program_id(0) * tm
    bnd = jnp.where(r0 < lat_rows, (r0 // seq + 1) * seq, r0 + tm) - r0
    row = lax.broadcasted_iota(jnp.int32, acc.shape, 0)
    gate = jnp.where(row < bnd, ma_ref[...], mb_ref[...])
    o_ref[...] = x_ref[...] + gate * acc


def _mm_out_residual(a, w, x, modt, l, seq, lat_rows):
    rows, k = a.shape
    n = w.shape[-1]
    tm = _pick_tm(rows, MM_TM_MAX)
    tn = 2 * MM_TN if n % (2 * MM_TN) == 0 else MM_TN

    def gate_spec(last):
        def index(i, j):
            r = i * tm + (tm - 1 if last else 0)
            return (l, _row_segment(r, seq, lat_rows) * N_MOD + 2, 0, j)

        return pl.BlockSpec((None, None, 1, tn), index)

    return pl.pallas_call(
        functools.partial(_mm_out_kernel, seq=seq, lat_rows=lat_rows),
        grid=(rows // tm, n // tn),
        in_specs=[pl.BlockSpec((tm, k), lambda i, j: (i, 0)),
                  pl.BlockSpec((None, k, tn), lambda i, j: (l, 0, j)),
                  pl.BlockSpec((tm, tn), lambda i, j: (i, j)),
                  gate_spec(False), gate_spec(True)],
        out_specs=pl.BlockSpec((tm, tn), lambda i, j: (i, j)),
        out_shape=jax.ShapeDtypeStruct(x.shape, F32),
        input_output_aliases={2: 0},
        compiler_params=_cp("arbitrary", "arbitrary"),
        name="matmul_out_residual",
    )(a, w, x, modt, modt)


def _merge_kernel(ya_ref, yb_ref, yc_ref, wa_ref, wb_ref, wc_ref, ga_ref, gb_ref, gc_ref, o_ref):
    def branch(y_ref, w_ref, g_ref):
        return g_ref[...].astype(F32) * jnp.dot(y_ref[...], w_ref[...].astype(BF16),
                                                preferred_element_type=F32)

    acc = branch(ya_ref, wa_ref, ga_ref) + branch(yb_ref, wb_ref, gb_ref) + branch(yc_ref, wc_ref, gc_ref)
    o_ref[...] = acc.astype(o_ref.dtype)


def _merge(ya, yb, yc, wpa, wpb, wpc, gates, l):
    rows, k = yb.shape
    d = wpa.shape[-1]
    tm = _pick_tm(rows, MM_TM_MAX)
    tn = MM_TN
    nd = d // tn
    a_spec = pl.BlockSpec((tm, k), lambda i, j: (i, 0))
    w_spec = pl.BlockSpec((None, k, tn), lambda i, j: (l, 0, j))

    def g_spec(br):
        return pl.BlockSpec((tm, tn), lambda i, j: (i, br * nd + j))

    return pl.pallas_call(
        _merge_kernel,
        grid=(rows // tm, nd),
        in_specs=[a_spec, a_spec, a_spec, w_spec, w_spec, w_spec, g_spec(0), g_spec(1), g_spec(2)],
        out_specs=pl.BlockSpec((tm, tn), lambda i, j: (i, j)),
        out_shape=jax.ShapeDtypeStruct((rows, d), BF16),
        compiler_params=_cp("arbitrary", "arbitrary"),
        name="merge_branches",
    )(ya, yb, yc, wpa, wpb, wpc, gates, gates, gates)


def _rglru_coeffs(x_ref, prev_ref, next_ref, cw_ref, cb_ref, wr_ref, br_ref, wi_ref, bi_ref, lam_ref,
                  a_scr, b_scr, r_scr, i_scr, first, last):
    tc, w = a_scr.shape
    hd = w // H_A
    x = x_ref[...].astype(F32)
    row = lax.broadcasted_iota(jnp.int32, (tc, w), 0)
    zero_row = jnp.zeros((1, w), F32)
    prev = prev_ref[...].astype(F32)
    nxt = next_ref[...].astype(F32)
    p_last = jnp.where(first, zero_row, prev[15:16, :])
    n0 = jnp.where(last, zero_row, nxt[0:1, :])
    n1 = jnp.where(last, zero_row, nxt[1:2, :])
    x_m1 = jnp.where(row == 0, p_last, pltpu.roll(x, 1, axis=0))
    x_p1 = jnp.where(row == tc - 1, n0, pltpu.roll(x, tc - 1, axis=0))
    x_p2 = jnp.where(row == tc - 2, n0, jnp.where(row == tc - 1, n1, pltpu.roll(x, tc - 2, axis=0)))
    xc = (cw_ref[0:1, :] * x_m1 + cw_ref[1:2, :] * x + cw_ref[2:3, :] * x_p1
          + cw_ref[3:4, :] * x_p2 + cb_ref[...])
    xb = xc.astype(BF16)
    for h in range(H_A):
        sl = slice(h * hd, (h + 1) * hd)
        xh = xb[:, sl]
        r_scr[:, sl] = jnp.dot(xh, wr_ref[h].astype(BF16), preferred_element_type=F32)
        i_scr[:, sl] = jnp.dot(xh, wi_ref[h].astype(BF16), preferred_element_type=F32)
    r = _sigmoid(r_scr[...] + br_ref[...])
    ig = _sigmoid(i_scr[...] + bi_ref[...])
    nl = -lam_ref[...]
    softplus = jnp.maximum(nl, 0.0) + jnp.log1p(jnp.exp(-jnp.abs(nl)))
    log_a = (-LRU_C) * r * softplus
    a_scr[...] = jnp.exp(log_a)
    t = jnp.tanh(log_a)
    one_m = (-2.0 * t) / (1.0 - t)
    b_scr[...] = jnp.sqrt(one_m) * (ig * xc)


def _block_scan(a, b, reverse):
    row = lax.broadcasted_iota(jnp.int32, a.shape, 0)
    for d in (1, 2, 4):
        if reverse:
            keep = row < SUB - d
            shift = SUB - d
        else:
            keep = row >= d
            shift = d
        a_sh = jnp.where(keep, pltpu.roll(a, shift, axis=0), 1.0)
        b_sh = jnp.where(keep, pltpu.roll(b, shift, axis=0), 0.0)
        b = a * b_sh + b
        a = a * a_sh
    return a, b


def _rglru_scan(a_scr, b_scr, h_scr, emit, reverse):
    tc, w = a_scr.shape
    nblk = tc // SUB

    def body(k, h):
        kk = (nblk - 1 - k) if reverse else k
        rows = pl.ds(pl.multiple_of(kk * SUB, SUB), SUB)
        ap, bp = _block_scan(a_scr[rows, :], b_scr[rows, :], reverse)
        out = ap * h + bp
        emit(rows, out)
        edge = out[0:1, :] if reverse else out[SUB - 1:SUB, :]
        return jnp.broadcast_to(edge, (SUB, w))

    h_scr[...] = lax.fori_loop(0, nblk, body, h_scr[...])


def _rglru_fwd_kernel(x_ref, prev_ref, next_ref, cw_ref, cb_ref, wr_ref, br_ref, wi_ref, bi_ref, lam_ref,
                      hf_ref, a_scr, b_scr, r_scr, i_scr, h_scr, *, bpb):
    j = pl.program_id(1)

    @pl.when(j == 0)
    def _():
        h_scr[...] = jnp.zeros_like(h_scr)

    first = jnp.logical_or(j == 0, j == 1)
    last = jnp.logical_or(j == 0, j == bpb)
    _rglru_coeffs(x_ref, prev_ref, next_ref, cw_ref, cb_ref, wr_ref, br_ref, wi_ref, bi_ref, lam_ref,
                  a_scr, b_scr, r_scr, i_scr, first, last)

    def emit(rows, out):
        hf_ref[rows, :] = out

    _rglru_scan(a_scr, b_scr, h_scr, emit, reverse=False)


def _rglru_bwd_kernel(x_ref, prev_ref, next_ref, cw_ref, cb_ref, wr_ref, br_ref, wi_ref, bi_ref, lam_ref,
                      hf_ref, ga_ref, ya_ref, a_scr, b_scr, r_scr, i_scr, h_scr, *, bpb):
    j = pl.program_id(1)

    @pl.when(j == 0)
    def _():
        h_scr[...] = jnp.zeros_like(h_scr)

    first = jnp.logical_or(j == 0, j == bpb)
    last = jnp.logical_or(j == 0, j == 1)
    _rglru_coeffs(x_ref, prev_ref, next_ref, cw_ref, cb_ref, wr_ref, br_ref, wi_ref, bi_ref, lam_ref,
                  a_scr, b_scr, r_scr, i_scr, first, last)

    def emit(rows, out):
        ya_ref[rows, :] = ((hf_ref[rows, :] + out) * ga_ref[rows, :].astype(F32)).astype(ya_ref.dtype)

    _rglru_scan(a_scr, b_scr, h_scr, emit, reverse=True)


def _rglru(xa, ga, p, l, nbatch, bpb):
    rows, w = xa.shape
    hd = w // H_A
    lat_blocks = nbatch * bpb
    ga_blocks = ga.shape[0] // SEG
    nhalo = rows // 16
    per_blk = SEG // 16

    def common_specs(direction, blk):
        def prev_map(b, j):
            return (jnp.maximum(blk(b, j) * per_blk - 1, 0), 0)

        def next_map(b, j):
            return (jnp.minimum((blk(b, j) + 1) * per_blk, nhalo - 1), 0)

        vec = pl.BlockSpec((None, None, 1, w), lambda b, j: (l, direction, 0, 0))
        gw = pl.BlockSpec((None, None, H_A, hd, hd), lambda b, j: (l, direction, 0, 0, 0))
        return [pl.BlockSpec((SEG, w), lambda b, j: (blk(b, j), 0)),
                pl.BlockSpec((16, w), prev_map),
                pl.BlockSpec((16, w), next_map),
                pl.BlockSpec((None, CONV_A, w), lambda b, j: (l, 0, 0)),
                pl.BlockSpec((None, 1, w), lambda b, j: (l, 0, 0)),
                gw, vec, gw, vec, vec]

    def blk_f(b, j):
        return jnp.where(j == 0, lat_blocks + b, b * bpb + j - 1)

    def blk_b(b, j):
        return jnp.where(j == 0, lat_blocks + b, b * bpb + bpb - j)

    scratch = [pltpu.VMEM((SEG, w), F32)] * 4 + [pltpu.VMEM((SUB, w), F32)]
    weights = (p["rg_conv_w"], p["rg_conv_b"], p["rg_w_r"], p["rg_b_r"], p["rg_w_i"], p["rg_b_i"], p["rg_lambda"])
    hf = pl.pallas_call(
        functools.partial(_rglru_fwd_kernel, bpb=bpb),
        grid=(nbatch, bpb + 1),
        in_specs=common_specs(0, blk_f),
        out_specs=pl.BlockSpec((SEG, w), lambda b, j: (blk_f(b, j), 0)),
        out_shape=jax.ShapeDtypeStruct((rows, w), F32),
        scratch_shapes=scratch,
        compiler_params=_cp("arbitrary", "arbitrary"),
        name="rglru_forward",
    )(xa, xa, xa, *weights)
    return pl.pallas_call(
        functools.partial(_rglru_bwd_kernel, bpb=bpb),
        grid=(nbatch, bpb + 1),
        in_specs=common_specs(1, blk_b) + [pl.BlockSpec((SEG, w), lambda b, j: (blk_b(b, j), 0)),
                                           pl.BlockSpec((SEG, w), lambda b, j: (jnp.minimum(blk_b(b, j), ga_blocks - 1), 0))],
        out_specs=pl.BlockSpec((SEG, w), lambda b, j: (blk_b(b, j), 0)),
        out_shape=jax.ShapeDtypeStruct((rows, w), BF16),
        scratch_shapes=scratch,
        compiler_params=_cp("arbitrary", "arbitrary"),
        name="rglru_backward",
    )(xa, xa, xa, *weights, hf, ga)


def _layernorm(x, g, b):
    mu = jnp.mean(x, axis=-1, keepdims=True)
    xc = x - mu
    var = jnp.mean(xc * xc, axis=-1, keepdims=True)
    return xc * lax.rsqrt(var + EPS) * g + b


def _sgu_kernel(u_ref, v_ref, g_ref, b_ref, ws_ref, bs_ref, o_ref):
    tc, w = u_ref.shape
    gw = w // G_B
    v = _layernorm(v_ref[...].astype(F32), g_ref[...], b_ref[...]).astype(BF16)
    for n in range(tc // CHUNK):
        rs = slice(n * CHUNK, (n + 1) * CHUNK)
        for g in range(G_B):
            cs = slice(g * gw, (g + 1) * gw)
            s = jnp.dot(ws_ref[g].astype(BF16), v[rs, cs], preferred_element_type=F32)
            o_ref[rs, cs] = (u_ref[rs, cs].astype(F32) * (s + bs_ref[:, cs])).astype(o_ref.dtype)


def _sgu(uv, p, l):
    rows, w2 = uv.shape
    w = w2 // 2
    depth = p["sg_b"].shape[0]
    bs = jnp.repeat(jnp.swapaxes(p["sg_b"], 1, 2), w // G_B, axis=2)
    vec = pl.BlockSpec((None, 1, w), lambda i: (l, 0, 0))
    return pl.pallas_call(
        _sgu_kernel,
        grid=(rows // SEG,),
        in_specs=[pl.BlockSpec((SEG, w), lambda i: (i, 0)),
                  pl.BlockSpec((SEG, w), lambda i: (i, 1)),
                  vec, vec,
                  pl.BlockSpec((None, G_B, CHUNK, CHUNK), lambda i: (l, 0, 0, 0)),
                  pl.BlockSpec((None, CHUNK, w), lambda i: (l, 0, 0))],
        out_specs=pl.BlockSpec((SEG, w), lambda i: (i, 0)),
        out_shape=jax.ShapeDtypeStruct((rows, w), BF16),
        compiler_params=_cp("arbitrary"),
        name="spatial_gating",
    )(uv, uv, p["sg_ln_g"].reshape(depth, 1, w), p["sg_ln_b"].reshape(depth, 1, w), p["sg_w"], bs)


def _conv_lat_kernel(x_ref, cw_ref, cb_ref, g_ref, b_ref, o_ref, xp, ys):
    nr, wc, c = x_ref.shape
    half = CONV_C // 2
    zeros = jnp.zeros((half, wc, c), F32)
    xp[0:half, :, 0:c] = zeros
    xp[half + nr:half + nr + half, :, 0:c] = zeros
    xp[half:half + nr, :, 0:c] = x_ref[...].astype(F32)

    for g in range(c // LANES):
        ls = slice(g * LANES, (g + 1) * LANES)
        wv = [jnp.broadcast_to(cw_ref[k:k + 1, ls], (SUB, LANES)) for k in range(CONV_C)]

        def taps(r, carry, ls=ls, wv=wv):
            for hs in range(wc // SUB):
                ss = slice(hs * SUB, (hs + 1) * SUB)
                ys[r, ss, ls] = _tree_sum([wv[k] * xp[r + k, ss, ls] for k in range(CONV_C)])
            return carry

        lax.fori_loop(0, nr, taps, 0, unroll=2)

    def finish(r, carry):
        y = _layernorm(ys[r] + cb_ref[...], g_ref[...], b_ref[...])
        o_ref[r] = _silu(y).astype(o_ref.dtype)
        return carry

    lax.fori_loop(0, nr, finish, 0, unroll=2)


def _conv_ctx_kernel(x_ref, cw_ref, cb_ref, g_ref, b_ref, o_ref, xp):
    n, c = x_ref.shape
    half = CONV_C // 2
    pad = 16
    xp[0:pad, :] = jnp.zeros((pad, c), F32)
    xp[pad + n:pad + n + pad, :] = jnp.zeros((pad, c), F32)
    xp[pad:pad + n, :] = x_ref[...].astype(F32)
    acc = cw_ref[0:1, :] * xp[pad - half:pad - half + n, :]
    for k in range(1, CONV_C):
        off = pad - half + k
        acc = acc + cw_ref[k:k + 1, :] * xp[off:off + n, :]
    y = _layernorm(acc + cb_ref[...], g_ref[...], b_ref[...])
    o_ref[...] = _silu(y).astype(o_ref.dtype)


def _conformer(glu, p, l, nbatch, seq, ctx_len, with_ctx):
    rows, c = glu.shape
    depth = p["cv_b"].shape[0]
    nrow = seq // GRID_W
    wt = 16
    cw = p["cv_w"]
    vecs = [p["cv_b"].reshape(depth, 1, c), p["cv_ln_g"].reshape(depth, 1, c), p["cv_ln_b"].reshape(depth, 1, c)]
    g3 = glu.reshape(rows // GRID_W, GRID_W, c)
    half = CONV_C // 2

    lat = pl.pallas_call(
        _conv_lat_kernel,
        grid=(nbatch, GRID_W // wt),
        in_specs=[pl.BlockSpec((nrow, wt, c), lambda b, j: (b, j, 0)),
                  pl.BlockSpec((None, CONV_C, c), lambda b, j: (l, 0, 0))]
                 + [pl.BlockSpec((None, 1, c), lambda b, j: (l, 0, 0))] * 3,
        out_specs=pl.BlockSpec((nrow, wt, c), lambda b, j: (b, j, 0)),
        out_shape=jax.ShapeDtypeStruct((nbatch * nrow, GRID_W, c), BF16),
        scratch_shapes=[pltpu.VMEM((nrow + 2 * half, wt, c + LANES), F32), pltpu.VMEM((nrow, wt, c), F32)],
        compiler_params=_cp("arbitrary", "arbitrary"),
        name="conformer_conv_latent",
    )(g3, cw, *vecs)
    lat = lat.reshape(nbatch * seq, c)
    if not with_ctx:
        return lat
    cblk0 = nbatch * seq // ctx_len
    cx = pl.pallas_call(
        _conv_ctx_kernel,
        grid=(nbatch,),
        in_specs=[pl.BlockSpec((ctx_len, c), lambda b: (cblk0 + b, 0)),
                  pl.BlockSpec((None, CONV_C, c), lambda b: (l, 0, 0))]
                 + [pl.BlockSpec((None, 1, c), lambda b: (l, 0, 0))] * 3,
        out_specs=pl.BlockSpec((ctx_len, c), lambda b: (b, 0)),
        out_shape=jax.ShapeDtypeStruct((nbatch * ctx_len, c), BF16),
        scratch_shapes=[pltpu.VMEM((ctx_len + 32, c), F32)],
        compiler_params=_cp("arbitrary"),
        name="conformer_conv_context",
    )(glu, cw, *vecs)
    return jnp.concatenate([lat, cx], axis=0)


def _ffn_pre_kernel(x_ref, g_ref, sh_ref, sc_ref, wr_ref, hx_ref, aff_ref):
    d = x_ref.shape[1]
    dh = d // 2
    x = x_ref[...]
    y = x * lax.rsqrt(jnp.mean(x * x, axis=-1, keepdims=True) + EPS) * g_ref[...]
    h = y * (1.0 + sc_ref[...]) + sh_ref[...]
    w = wr_ref[...]
    h_hi = h.astype(BF16)
    h_lo = (h - h_hi.astype(F32)).astype(BF16)
    w_hi = w.astype(BF16)
    w_lo = (w - w_hi.astype(F32)).astype(BF16)
    logits = (jnp.dot(h_hi, w_hi, preferred_element_type=F32)
              + (jnp.dot(h_lo, w_hi, preferred_element_type=F32)
                 + jnp.dot(h_hi, w_lo, preferred_element_type=F32)))
    lane = lax.broadcasted_iota(jnp.int32, logits.shape, 1)
    valid = lane < N_EXP
    m = jnp.max(jnp.where(valid, logits, -jnp.inf), axis=-1, keepdims=True)
    ex = jnp.where(valid, jnp.exp(logits - m), 0.0)
    aff = ex / jnp.sum(ex, axis=-1, keepdims=True)
    aff_ref[...] = aff
    lo = pltpu.bitcast(h[:, :dh].astype(BF16).astype(F32), jnp.int32)
    hi = pltpu.bitcast(h[:, dh:].astype(BF16).astype(F32), jnp.int32)
    hx_ref[:, :dh] = jnp.bitwise_or(jnp.bitwise_and(hi, jnp.int32(-65536)),
                                    lax.shift_right_logical(lo, jnp.int32(16)))
    hx_ref[:, dh:] = pltpu.bitcast(aff, jnp.int32)


def _ffn_pre(x, rows, g, modt, wr_pad, l, lat_blocks, bpb):
    d = x.shape[1]
    blk = pl.BlockSpec((SEG, d), lambda i: (i, 0))

    def mod_spec(k):
        return pl.BlockSpec((None, None, 1, d),
                            lambda i: (l, _seg_of_block(i, lat_blocks, bpb) * N_MOD + k, 0, 0))

    return pl.pallas_call(
        _ffn_pre_kernel,
        grid=(rows // SEG,),
        in_specs=[blk,
                  pl.BlockSpec((None, 1, d), lambda i: (l, 0, 0)),
                  mod_spec(3), mod_spec(4),
                  pl.BlockSpec((None, d, LANES), lambda i: (l, 0, 0))],
        out_specs=[pl.BlockSpec((SEG, d // 2 + LANES), lambda i: (i, 0)),
                   pl.BlockSpec((SEG, LANES), lambda i: (i, 0))],
        out_shape=[jax.ShapeDtypeStruct((rows, d // 2 + LANES), jnp.int32),
                   jax.ShapeDtypeStruct((rows, LANES), F32)],
        compiler_params=_cp("arbitrary"),
        name="ffn_pre_router",
    )(x, g, modt, modt, wr_pad)


def _prefix_mats():
    r = lax.broadcasted_iota(jnp.int32, (LANES, LANES), 0)
    c = lax.broadcasted_iota(jnp.int32, (LANES, LANES), 1)
    incl = jnp.where(r <= c, 1.0, 0.0).astype(BF16)
    strict = jnp.where(c < r, 1.0, 0.0).astype(BF16)
    return incl, strict


def _topk_kernel(a_ref, o_ref, thr_scr, *, cap):
    nexp, nc, _ = a_ref.shape
    bits_all = pltpu.bitcast(a_ref[...], jnp.int32)
    thr_all = jnp.zeros((nexp, 1, 1), jnp.int32)
    for bit in range(30, -1, -1):
        cand = thr_all | jnp.int32(1 << bit)
        ge = jnp.where(bits_all >= cand, 1.0, 0.0)
        cnt_all = jnp.sum(jnp.sum(ge, axis=2, keepdims=True), axis=1, keepdims=True)
        thr_all = jnp.where(cnt_all >= float(cap), cand, thr_all)
    thr_scr[...] = jnp.broadcast_to(thr_all, thr_scr.shape)
    incl, strict = _prefix_mats()
    ones_sq = jnp.ones((LANES, LANES), BF16)
    ones8 = jnp.ones((SUB, LANES), BF16)
    lane_c = lax.broadcasted_iota(jnp.int32, (cap, LANES), 1)
    s_col = lax.broadcasted_iota(jnp.int32, (cap, LANES), 0).astype(F32)
    capf = jnp.float32(cap)
    o_ref[...] = jnp.zeros_like(o_ref)

    def total(x):
        return jnp.sum(jnp.sum(x, axis=1, keepdims=True), axis=0, keepdims=True)

    def pad_rows(x):
        if nc == LANES:
            return x
        return jnp.concatenate([x, jnp.zeros((LANES - nc, LANES), x.dtype)], axis=0)

    def prefix(mask_f):
        mp = pad_rows(mask_f).astype(BF16)
        within = jnp.dot(mp, incl, preferred_element_type=F32)
        tot = jnp.dot(mp, ones_sq, preferred_element_type=F32).astype(BF16)
        excl = jnp.dot(strict, tot, preferred_element_type=F32)
        return mp, within, within + excl

    def body(e, carry):
        bits = pltpu.bitcast(a_ref[e], jnp.int32)
        thr = thr_scr[e][0:1, :]
        gt = bits > thr
        eq = bits == thr
        need = capf - total(jnp.where(gt, 1.0, 0.0))
        _, _, pe = prefix(jnp.where(eq, 1.0, 0.0))
        take = jnp.logical_or(gt, jnp.logical_and(eq, pe[:nc] <= need))
        mp, within, _ = prefix(jnp.where(take, 1.0, 0.0))
        s_row = lax.dot_general(ones8, mp, (((1,), (1,)), ((), ())), preferred_element_type=F32)
        pend_row = jnp.dot(s_row.astype(BF16), incl, preferred_element_type=F32)
        pend_b = jnp.broadcast_to(pend_row[0:1, :], (cap, LANES))
        s_b = jnp.broadcast_to(s_row[0:1, :], (cap, LANES))
        before = jnp.logical_and(pend_b <= s_col, lane_c < nc)
        c_s = jnp.sum(jnp.where(before, 1.0, 0.0), axis=1, keepdims=True)
        p_excl = jnp.sum(jnp.where(before, s_b, 0.0), axis=1, keepdims=True)
        sel = jnp.where(lane_c.astype(F32) == c_s, 1.0, 0.0).astype(BF16)
        w_row = jnp.dot(sel, within.astype(BF16), preferred_element_type=F32)
        cnt = jnp.sum(jnp.where(w_row <= s_col - p_excl, 1.0, 0.0), axis=1, keepdims=True)
        idx = (c_s * float(LANES) + cnt).astype(jnp.int32)
        o_ref[...] = jnp.where(lane_c == e, idx, o_ref[...])
        return carry

    lax.fori_loop(0, nexp, body, 0)


def _topk(aff_t, cap):
    nsets, nexp, nc, _ = aff_t.shape
    return pl.pallas_call(
        functools.partial(_topk_kernel, cap=cap),
        grid=(nsets,),
        in_specs=[pl.BlockSpec((None, nexp, nc, LANES), lambda s: (s, 0, 0, 0))],
        out_specs=pl.BlockSpec((None, cap, LANES), lambda s: (s, 0, 0)),
        out_shape=jax.ShapeDtypeStruct((nsets, cap, LANES), jnp.int32),
        scratch_shapes=[pltpu.VMEM((nexp, SUB, LANES), jnp.int32)],
        compiler_params=_cp("arbitrary"),
        name="expert_choice_topk",
    )(aff_t)


def _expert_kernel(idx_ref, hx_hbm, wg_ref, wu_ref, wd_ref, *refs, nf, segs):
    nseg = len(segs)
    gm_refs = refs[:nseg]
    o_ref, xg, xlo, xhi, hid, gate, sem = refs[nseg:]
    e = pl.program_id(0)
    s = pl.program_id(1)
    nr = xg.shape[0] * SUB
    dh = xlo.shape[1]
    tf = wg_ref.shape[1]

    def row_copy(g, u, row):
        return pltpu.make_async_copy(hx_hbm.at[pl.ds(row, 1)], xg.at[g, pl.ds(u, 1)], sem)

    @pl.when(s == 0)
    def _():
        def issue(g, c):
            for u in range(SUB):
                row_copy(g, u, idx_ref[e * nr + g * SUB + u]).start()
            return c

        lax.fori_loop(0, nr // SUB, issue, 0)

        def drain(g, c):
            for u in range(SUB):
                row_copy(g, u, 0).wait()
            return c

        lax.fori_loop(0, nr // SUB, drain, 0)

        def unpack(k, c):
            rows = pl.ds(pl.multiple_of(k * ROW_CHUNK, ROW_CHUNK), ROW_CHUNK)
            gchunk = ROW_CHUNK // SUB
            packed = xg[pl.ds(k * gchunk, gchunk)].reshape(ROW_CHUNK, xg.shape[2])
            u = packed[:, :dh]
            xlo[rows, :] = pltpu.bitcast(lax.shift_left(u, jnp.int32(16)), F32).astype(BF16)
            xhi[rows, :] = pltpu.bitcast(jnp.bitwise_and(u, jnp.int32(-65536)), F32).astype(BF16)
            aff = pltpu.bitcast(packed[:, dh:], F32)
            lane = lax.broadcasted_iota(jnp.int32, aff.shape, 1)
            g = jnp.sum(jnp.where(lane == e, aff, 0.0), axis=1, keepdims=True)
            gate[rows, :] = jnp.broadcast_to(g, aff.shape)
            return c

        lax.fori_loop(0, nr // ROW_CHUNK, unpack, 0, unroll=2)

    @pl.when(s < nf)
    def _():
        def proj(w_ref):
            return (jnp.dot(xlo[...], w_ref[:dh, :].astype(BF16), preferred_element_type=F32)
                    + jnp.dot(xhi[...], w_ref[dh:, :].astype(BF16), preferred_element_type=F32))

        hid[s] = (_silu(proj(wg_ref)) * proj(wu_ref)).astype(BF16)

    @pl.when(s >= nf)
    def _():
        acc = jnp.dot(hid[0], wd_ref[0:tf, :].astype(BF16), preferred_element_type=F32)
        for f in range(1, nf):
            acc = acc + jnp.dot(hid[f], wd_ref[f * tf:(f + 1) * tf, :].astype(BF16),
                                preferred_element_type=F32)
        for (r0, r1), gm_ref in zip(segs, gm_refs):
            for k in range(o_ref.shape[1] // LANES):
                cs = slice(k * LANES, (k + 1) * LANES)
                o_ref[r0:r1, cs] = (acc[r0:r1, cs] * gate[r0:r1, :] * gm_ref[:, cs]).astype(o_ref.dtype)


def _experts(idx_flat, hx, w_g, w_u, w_d, modt, l, nr, segs, seg_ids):
    d, dexp = w_g.shape[-2:]
    tf = 256 if dexp % 256 == 0 else dexp
    td = 512 if d % 512 == 0 else d
    nf = dexp // tf
    nd = d // td
    wcols = hx.shape[1]

    def gm_spec(seg):
        return pl.BlockSpec((None, None, 1, td),
                            lambda e, s, idx: (l, seg * N_MOD + 5, 0, jnp.maximum(s - nf, 0)))

    grid_spec = pltpu.PrefetchScalarGridSpec(
        num_scalar_prefetch=1,
        grid=(N_EXP, nf + nd),
        in_specs=[pl.BlockSpec(memory_space=pl.ANY),
                  pl.BlockSpec((None, None, d, tf), lambda e, s, idx: (l, e, 0, jnp.minimum(s, nf - 1))),
                  pl.BlockSpec((None, None, d, tf), lambda e, s, idx: (l, e, 0, jnp.minimum(s, nf - 1))),
                  pl.BlockSpec((None, None, dexp, td), lambda e, s, idx: (l, e, 0, jnp.maximum(s - nf, 0)))]
                 + [gm_spec(sg) for sg in seg_ids],
        out_specs=pl.BlockSpec((None, nr, td), lambda e, s, idx: (e, 0, jnp.maximum(s - nf, 0))),
        scratch_shapes=[pltpu.VMEM((nr // SUB, SUB, wcols), jnp.int32),
                        pltpu.VMEM((nr, d // 2), BF16),
                        pltpu.VMEM((nr, d // 2), BF16),
                        pltpu.VMEM((nf, nr, tf), BF16),
                        pltpu.VMEM((nr, LANES), F32),
                        pltpu.SemaphoreType.DMA(())],
    )
    return pl.pallas_call(
        functools.partial(_expert_kernel, nf=nf, segs=segs),
        grid_spec=grid_spec,
        out_shape=jax.ShapeDtypeStruct((N_EXP, nr, d), BF16),
        compiler_params=_cp("arbitrary", "arbitrary"),
        name="experts_swiglu",
    )(idx_flat, hx, w_g, w_u, w_d, *([modt] * len(seg_ids)))


def _scatter_kernel(idx_ref, ye_ref, x_in, x_out, buf, rsem, wsem):
    del x_in
    e = pl.program_id(0)
    ngroups, _, d = buf.shape
    nr = ngroups * SUB
    gchunk = ROW_CHUNK // SUB

    def rd(g, u, row):
        return pltpu.make_async_copy(x_out.at[pl.ds(row, 1)], buf.at[g, pl.ds(u, 1)], rsem)

    def wr(g, u, row):
        return pltpu.make_async_copy(buf.at[g, pl.ds(u, 1)], x_out.at[pl.ds(row, 1)], wsem)

    def each(fn):
        def body(g, carry):
            for u in range(SUB):
                fn(g, u)
            return carry

        lax.fori_loop(0, ngroups, body, 0)

    each(lambda g, u: rd(g, u, idx_ref[e * nr + g * SUB + u]).start())
    each(lambda g, u: rd(g, u, 0).wait())

    def add(k, carry):
        rows = pl.ds(pl.multiple_of(k * ROW_CHUNK, ROW_CHUNK), ROW_CHUNK)
        groups = pl.ds(k * gchunk, gchunk)
        buf[groups] = buf[groups] + ye_ref[rows, :].astype(F32).reshape(gchunk, SUB, d)
        return carry

    lax.fori_loop(0, nr // ROW_CHUNK, add, 0)
    each(lambda g, u: wr(g, u, idx_ref[e * nr + g * SUB + u]).start())
    each(lambda g, u: wr(g, u, 0).wait())


def _scatter_add(idx_flat, ye, x):
    nexp, nr, d = ye.shape
    grid_spec = pltpu.PrefetchScalarGridSpec(
        num_scalar_prefetch=1,
        grid=(nexp,),
        in_specs=[pl.BlockSpec((None, nr, d), lambda e, idx: (e, 0, 0)),
                  pl.BlockSpec(memory_space=pl.ANY)],
        out_specs=pl.BlockSpec(memory_space=pl.ANY),
        scratch_shapes=[pltpu.VMEM((nr // SUB, SUB, d), F32),
                        pltpu.SemaphoreType.DMA(()), pltpu.SemaphoreType.DMA(())],
    )
    return pl.pallas_call(
        _scatter_kernel,
        grid_spec=grid_spec,
        out_shape=jax.ShapeDtypeStruct(x.shape, x.dtype),
        input_output_aliases={2: 0},
        compiler_params=_cp("arbitrary"),
        name="scatter_add_residual",
    )(idx_flat, ye, x)


def _final_norm_kernel(x_ref, g_ref, o_ref):
    x = x_ref[...]
    o_ref[...] = x * lax.rsqrt(jnp.mean(x * x, axis=-1, keepdims=True) + EPS) * g_ref[...]


def _final_norm(x, g, nrows):
    d = x.shape[1]
    return pl.pallas_call(
        _final_norm_kernel,
        grid=(nrows // SEG,),
        in_specs=[pl.BlockSpec((SEG, d), lambda i: (i, 0)), pl.BlockSpec((1, d), lambda i: (0, 0))],
        out_specs=pl.BlockSpec((SEG, d), lambda i: (i, 0)),
        out_shape=jax.ShapeDtypeStruct((nrows, d), F32),
        compiler_params=_cp("arbitrary"),
        name="final_norm",
    )(x, g.reshape(1, d))


def _route(aff, nbatch, seq, ctx_len, with_ctx):
    lat = nbatch * seq
    cap_l = CAP_FACTOR * seq // N_EXP
    cap_c = CAP_FACTOR * ctx_len // N_EXP
    a_l = aff[:lat, :N_EXP].reshape(nbatch, seq, N_EXP).transpose(0, 2, 1)
    a_l = a_l.reshape(nbatch, N_EXP, seq // LANES, LANES)
    idx_l = _topk(a_l, cap_l)[:, :, :N_EXP]
    idx_l = idx_l + (jnp.arange(nbatch, dtype=jnp.int32) * seq)[:, None, None]
    segs = tuple((b * cap_l, (b + 1) * cap_l) for b in range(nbatch))
    seg_ids = tuple(range(nbatch))
    if not with_ctx:
        idx = idx_l.transpose(2, 0, 1).reshape(N_EXP, nbatch * cap_l)
        return idx.reshape(-1), idx.shape[1], segs, seg_ids
    cpad = SUB * LANES
    a_c = aff[lat:, :N_EXP].reshape(nbatch, ctx_len, N_EXP).transpose(0, 2, 1)
    a_c = jnp.pad(a_c, ((0, 0), (0, 0), (0, cpad - ctx_len)), constant_values=-1.0)
    a_c = a_c.reshape(nbatch, N_EXP, SUB, LANES)
    idx_c = _topk(a_c, cap_c)[:, :, :N_EXP]
    idx_c = idx_c + (lat + jnp.arange(nbatch, dtype=jnp.int32) * ctx_len)[:, None, None]
    idx = jnp.concatenate([idx_l.transpose(2, 0, 1).reshape(N_EXP, nbatch * cap_l),
                           idx_c.transpose(2, 0, 1).reshape(N_EXP, nbatch * cap_c)], axis=1)
    segs = segs + ((nbatch * cap_l, nbatch * (cap_l + cap_c)),)
    seg_ids = seg_ids + (nbatch,)
    return idx.reshape(-1), idx.shape[1], segs, seg_ids


def kernel(x, c, ctx, c_ctx, w_mod, b_mod, g_mix, g_ffn, w_in, b_in, rg_conv_w, rg_conv_b, rg_w_r, rg_b_r, rg_w_i, rg_b_i, rg_lambda, sg_ln_g, sg_ln_b, sg_w, sg_b, cv_w, cv_b, cv_ln_g, cv_ln_b, w_proj_a, w_proj_b, w_proj_c, w_out, w_router, w_e_gate, w_e_up, w_e_down, g_final):
    nbatch, seq, d = x.shape
    ctx_len = ctx.shape[1]
    depth = w_mod.shape[0]
    w_a = rg_conv_w.shape[-1]
    w_b = sg_ln_g.shape[-1]
    w_c = cv_w.shape[-1]
    assert ctx_len == SEG and seq % SEG == 0 and nbatch + 1 <= SUB
    bpb = seq // SEG
    lat_blocks = nbatch * bpb
    lat_rows = nbatch * seq
    all_rows = lat_rows + nbatch * ctx_len
    off_xa = w_a
    off_uv = 2 * w_a
    off_glu = off_uv + 2 * w_b
    off_gate = off_glu + 2 * w_c

    cvec = jnp.concatenate([c, c_ctx[None], jnp.zeros((SUB - nbatch - 1, d), F32)], axis=0)
    modt = _mod_all(cvec, w_mod, b_mod).reshape(depth, SUB * N_MOD, 1, d)

    b_in3 = b_in.reshape(depth, 1, -1)
    gmix3 = g_mix.reshape(depth, 1, d)
    gffn3 = g_ffn.reshape(depth, 1, d)
    wr_pad = jnp.pad(w_router, ((0, 0), (0, 0), (0, LANES - N_EXP)))
    rg = {"rg_conv_w": rg_conv_w, "rg_conv_b": rg_conv_b.reshape(depth, 1, w_a),
          "rg_w_r": rg_w_r, "rg_b_r": rg_b_r.reshape(depth, 2, 1, w_a),
          "rg_w_i": rg_w_i, "rg_b_i": rg_b_i.reshape(depth, 2, 1, w_a),
          "rg_lambda": rg_lambda.reshape(depth, 2, 1, w_a)}
    sg = {"sg_ln_g": sg_ln_g, "sg_ln_b": sg_ln_b, "sg_w": sg_w, "sg_b": sg_b}
    cv = {"cv_w": cv_w, "cv_b": cv_b, "cv_ln_g": cv_ln_g, "cv_ln_b": cv_ln_b}

    for l in range(depth):
        need_ctx = l < depth - 1
        nrows = all_rows if need_ctx else lat_rows
        if l == 0:
            xall, h = _first_norm_mod(x.reshape(lat_rows, d), ctx.reshape(all_rows - lat_rows, d),
                                      gmix3, modt, l, lat_blocks, bpb)
        else:
            h = _norm_mod(xall, gmix3, modt, l, 0, 1, lat_blocks, bpb)
        ga =_mm(h, w_in, b_in3, l, 0, w_a, _gelu, BF16, nrows)
        xa = _mm(h, w_in, b_in3, l, off_xa, w_a, lambda z: z, BF16, all_rows)
        uv = _mm(h, w_in, b_in3, l, off_uv, 2 * w_b, _gelu, BF16, nrows)
        glu = _mm_glu(h, w_in, b_in3, l, off_glu, w_c, nrows)
        gates = _mm(h, w_in, b_in3, l, off_gate, N_BRANCH * d, _sigmoid, BF16, nrows)
        ya = _rglru(xa, ga, rg, l, nbatch, bpb)
        yb = _sgu(uv, sg, l)
        yc = _conformer(glu, cv, l, nbatch, seq, ctx_len, need_ctx)
        y = _merge(ya, yb, yc, w_proj_a, w_proj_b, w_proj_c, gates, l)
        xall = _mm_out_residual(y, w_out, xall, modt, l, seq, lat_rows)
        hx, aff = _ffn_pre(xall, nrows, gffn3, modt, wr_pad, l, lat_blocks, bpb)
        idx_flat, nr, segs, seg_ids = _route(aff, nbatch, seq, ctx_len, need_ctx)
        ye = _experts(idx_flat, hx, w_e_gate, w_e_up, w_e_down, modt, l, nr, segs, seg_ids)
        xall = _scatter_add(idx_flat, ye, xall)
    return _final_norm(xall, g_final, lat_rows).reshape(nbatch, seq, d)
```

```python
import functools

import jax
import jax.numpy as jnp
from jax import lax
from jax.experimental import pallas as pl
from jax.experimental.pallas import tpu as pltpu

GRID_W = 64
N_MOD = 6
H_A = 16
CONV_A = 4
LRU_C = 8.0
G_B = 16
CHUNK = 128
CONV_C = 31
N_BRANCH = 3
N_EXP = 16
CAP_FACTOR = 2
EPS = 1e-6

LANES = 128
SUB = 8
SEG = 256
VMEM_LIMIT = 56 * 1024 * 1024
MM_TN = 256
MM_TM_MAX = 1152
ROW_CHUNK = 32

BF16 = jnp.bfloat16
F32 = jnp.float32


def _cp(*sem):
    return pltpu.CompilerParams(dimension_semantics=sem, vmem_limit_bytes=VMEM_LIMIT)


def _pick_tm(rows, tm_max):
    k = 1
    while rows % k or (rows // k) > tm_max or (rows // k) % 16:
        k += 1
    return rows // k


def _gelu(x):
    return 0.5 * x * (1.0 + jnp.tanh(0.7978845608028654 * (x + 0.044715 * (x * x * x))))


def _sigmoid(x):
    return 1.0 / (1.0 + jnp.exp(-x))


def _silu(x):
    return x * _sigmoid(x)


def _tree_sum(terms):
    while len(terms) > 1:
        terms = [terms[i] + terms[i + 1] for i in range(0, len(terms) - 1, 2)] + (
            [terms[-1]] if len(terms) % 2 else [])
    return terms[0]


def _mod_kernel(a_ref, w_ref, b_ref, o_ref):
    a = _silu(a_ref[...]).astype(BF16)
    o_ref[...] = jnp.dot(a, w_ref[...].astype(BF16), preferred_element_type=F32) + b_ref[...]


def _mod_all(cvec, w_mod, b_mod):
    depth, d, nm = w_mod.shape
    tn = 1024 if nm % 1024 == 0 else nm
    return pl.pallas_call(
        _mod_kernel,
        grid=(depth, nm // tn),
        in_specs=[pl.BlockSpec((SUB, d), lambda l, j: (0, 0)),
                  pl.BlockSpec((None, d, tn), lambda l, j: (l, 0, j)),
                  pl.BlockSpec((None, 1, tn), lambda l, j: (l, 0, j))],
        out_specs=pl.BlockSpec((None, SUB, tn), lambda l, j: (l, 0, j)),
        out_shape=jax.ShapeDtypeStruct((depth, SUB, nm), F32),
        compiler_params=_cp("arbitrary", "arbitrary"),
        name="mod_vectors",
    )(cvec, w_mod, b_mod.reshape(depth, 1, nm))


def _norm_mod_kernel(x_ref, g_ref, sh_ref, sc_ref, o_ref):
    x = x_ref[...]
    y = x * lax.rsqrt(jnp.mean(x * x, axis=-1, keepdims=True) + EPS) * g_ref[...]
    o_ref[...] = (y * (1.0 + sc_ref[...]) + sh_ref[...]).astype(o_ref.dtype)


def _first_norm_mod_kernel(x_ref, c_ref, g_ref, sh_ref, sc_ref, xall_ref, h_ref, *, lat_blocks):
    def emit(src_ref):
        xall_ref[...] = src_ref[...]
        _norm_mod_kernel(src_ref, g_ref, sh_ref, sc_ref, h_ref)

    is_latent = pl.program_id(0) < lat_blocks
    pl.when(is_latent)(lambda: emit(x_ref))
    pl.when(jnp.logical_not(is_latent))(lambda: emit(c_ref))


def _first_norm_mod(x2, c2, g, modt, l, lat_blocks, bpb):
    d = x2.shape[1]
    rows = x2.shape[0] + c2.shape[0]
    cblocks = c2.shape[0] // SEG

    def mod_spec(k):
        return pl.BlockSpec((None, None, 1, d),
                            lambda i: (l, _seg_of_block(i, lat_blocks, bpb) * N_MOD + k, 0, 0))

    blk = pl.BlockSpec((SEG, d), lambda i: (i, 0))
    return pl.pallas_call(
        functools.partial(_first_norm_mod_kernel, lat_blocks=lat_blocks),
        grid=(rows // SEG,),
        in_specs=[pl.BlockSpec((SEG, d), lambda i: (jnp.minimum(i, lat_blocks - 1), 0)),
                  pl.BlockSpec((SEG, d), lambda i: (jnp.clip(i - lat_blocks, 0, cblocks - 1), 0)),
                  pl.BlockSpec((None, 1, d), lambda i: (l, 0, 0)),
                  mod_spec(0), mod_spec(1)],
        out_specs=[blk, blk],
        out_shape=[jax.ShapeDtypeStruct((rows, d), F32), jax.ShapeDtypeStruct((rows, d), BF16)],
        compiler_params=_cp("arbitrary"),
        name="embed_norm_modulate",
    )(x2, c2, g, modt, modt)


def _seg_of_block(i, lat_blocks, blocks_per_batch):
    return jnp.where(i < lat_blocks, i // blocks_per_batch, lat_blocks // blocks_per_batch)


def _norm_mod(x, g, modt, l, k_shift, k_scale, lat_blocks, bpb):
    rows, d = x.shape

    def mod_spec(k):
        return pl.BlockSpec((None, None, 1, d),
                            lambda i: (l, _seg_of_block(i, lat_blocks, bpb) * N_MOD + k, 0, 0))

    return pl.pallas_call(
        _norm_mod_kernel,
        grid=(rows // SEG,),
        in_specs=[pl.BlockSpec((SEG, d), lambda i: (i, 0)),
                  pl.BlockSpec((None, 1, d), lambda i: (l, 0, 0)),
                  mod_spec(k_shift), mod_spec(k_scale)],
        out_specs=pl.BlockSpec((SEG, d), lambda i: (i, 0)),
        out_shape=jax.ShapeDtypeStruct((rows, d), BF16),
        compiler_params=_cp("arbitrary"),
        name="norm_modulate",
    )(x, g, modt, modt)


def _mm_kernel(a_ref, w_ref, b_ref, o_ref, *, epilogue):
    acc = jnp.dot(a_ref[...], w_ref[...].astype(BF16), preferred_element_type=F32) + b_ref[...]
    o_ref[...] = epilogue(acc).astype(o_ref.dtype)


def _mm_glu_kernel(a_ref, wp_ref, wq_ref, bp_ref, bq_ref, o_ref):
    a = a_ref[...]
    p = jnp.dot(a, wp_ref[...].astype(BF16), preferred_element_type=F32) + bp_ref[...]
    q = jnp.dot(a, wq_ref[...].astype(BF16), preferred_element_type=F32) + bq_ref[...]
    o_ref[...] = (p * _sigmoid(q)).astype(o_ref.dtype)


def _mm(a, w, b, l, n0, n_len, epilogue, out_dtype, rows):
    k = a.shape[1]
    tm = _pick_tm(rows, MM_TM_MAX)
    tn = 2 * MM_TN if (n_len % (2 * MM_TN) == 0 and n0 % (2 * MM_TN) == 0) else MM_TN
    j0 = n0 // tn
    return pl.pallas_call(
        functools.partial(_mm_kernel, epilogue=epilogue),
        grid=(rows // tm, n_len // tn),
        in_specs=[pl.BlockSpec((tm, k), lambda i, j: (i, 0)),
                  pl.BlockSpec((None, k, tn), lambda i, j: (l, 0, j0 + j)),
                  pl.BlockSpec((None, 1, tn), lambda i, j: (l, 0, j0 + j))],
        out_specs=pl.BlockSpec((tm, tn), lambda i, j: (i, j)),
        out_shape=jax.ShapeDtypeStruct((rows, n_len), out_dtype),
        compiler_params=_cp("arbitrary", "arbitrary"),
        name="matmul_bias_act",
    )(a, w, b)


def _mm_glu(a, w, b, l, n0, n_half, rows):
    k = a.shape[1]
    tm = _pick_tm(rows, MM_TM_MAX)
    tn = MM_TN
    jp = n0 // tn
    jq = (n0 + n_half) // tn
    return pl.pallas_call(
        _mm_glu_kernel,
        grid=(rows // tm, n_half // tn),
        in_specs=[pl.BlockSpec((tm, k), lambda i, j: (i, 0)),
                  pl.BlockSpec((None, k, tn), lambda i, j: (l, 0, jp + j)),
                  pl.BlockSpec((None, k, tn), lambda i, j: (l, 0, jq + j)),
                  pl.BlockSpec((None, 1, tn), lambda i, j: (l, 0, jp + j)),
                  pl.BlockSpec((None, 1, tn), lambda i, j: (l, 0, jq + j))],
        out_specs=pl.BlockSpec((tm, tn), lambda i, j: (i, j)),
        out_shape=jax.ShapeDtypeStruct((rows, n_half), BF16),
        compiler_params=_cp("arbitrary", "arbitrary"),
        name="matmul_glu",
    )(a, w, w, b, b)


def _row_segment(r, seq, lat_rows):
    return jnp.where(r < lat_rows, r // seq, lat_rows // seq)


def _mm_out_kernel(a_ref, w_ref, x_ref, ma_ref, mb_ref, o_ref, *, seq, lat_rows):
    tm = x_ref.shape[0]
    acc = jnp.dot(a_ref[...], w_ref[...].astype(BF16), preferred_element_type=F32)
    r0 = pl.program_id(0) * tm
    bnd = jnp.where(r0 < lat_rows, (r0 // seq + 1) * seq, r0 + tm) - r0
    row = lax.broadcasted_iota(jnp.int32, acc.shape, 0)
    gate = jnp.where(row < bnd, ma_ref[...], mb_ref[...])
    o_ref[...] = x_ref[...] + gate * acc


def _mm_out_residual(a, w, x, modt, l, seq, lat_rows):
    rows, k = a.shape
    n = w.shape[-1]
    tm = _pick_tm(rows, MM_TM_MAX)
    tn = 2 * MM_TN if n % (2 * MM_TN) == 0 else MM_TN

    def gate_spec(last):
        def index(i, j):
            r = i * tm + (tm - 1 if last else 0)
            return (l, _row_segment(r, seq, lat_rows) * N_MOD + 2, 0, j)

        return pl.BlockSpec((None, None, 1, tn), index)

    return pl.pallas_call(
        functools.partial(_mm_out_kernel, seq=seq, lat_rows=lat_rows),
        grid=(rows // tm, n // tn),
        in_specs=[pl.BlockSpec((tm, k), lambda i, j: (i, 0)),
                  pl.BlockSpec((None, k, tn), lambda i, j: (l, 0, j)),
                  pl.BlockSpec((tm, tn), lambda i, j: (i, j)),
                  gate_spec(False), gate_spec(True)],
        out_specs=pl.BlockSpec((tm, tn), lambda i, j: (i, j)),
        out_shape=jax.ShapeDtypeStruct(x.shape, F32),
        input_output_aliases={2: 0},
        compiler_params=_cp("arbitrary", "arbitrary"),
        name="matmul_out_residual",
    )(a, w, x, modt, modt)


def _merge_kernel(ya_ref, yb_ref, yc_ref, wa_ref, wb_ref, wc_ref, ga_ref, gb_ref, gc_ref, o_ref):
    def branch(y_ref, w_ref, g_ref):
        return g_ref[...].astype(F32) * jnp.dot(y_ref[...], w_ref[...].astype(BF16),
                                                preferred_element_type=F32)

    acc = branch(ya_ref, wa_ref, ga_ref) + branch(yb_ref, wb_ref, gb_ref) + branch(yc_ref, wc_ref, gc_ref)
    o_ref[...] = acc.astype(o_ref.dtype)


def _merge(ya, yb, yc, wpa, wpb, wpc, gates, l):
    rows, k = yb.shape
    d = wpa.shape[-1]
    tm = _pick_tm(rows, MM_TM_MAX)
    tn = MM_TN
    nd = d // tn
    a_spec = pl.BlockSpec((tm, k), lambda i, j: (i, 0))
    w_spec = pl.BlockSpec((None, k, tn), lambda i, j: (l, 0, j))

    def g_spec(br):
        return pl.BlockSpec((tm, tn), lambda i, j: (i, br * nd + j))

    return pl.pallas_call(
        _merge_kernel,
        grid=(rows // tm, nd),
        in_specs=[a_spec, a_spec, a_spec, w_spec, w_spec, w_spec, g_spec(0), g_spec(1), g_spec(2)],
        out_specs=pl.BlockSpec((tm, tn), lambda i, j: (i, j)),
        out_shape=jax.ShapeDtypeStruct((rows, d), BF16),
        compiler_params=_cp("arbitrary", "arbitrary"),
        name="merge_branches",
    )(ya, yb, yc, wpa, wpb, wpc, gates, gates, gates)


def _rglru_coeffs(x_ref, prev_ref, next_ref, cw_ref, cb_ref, wr_ref, br_ref, wi_ref, bi_ref, lam_ref,
                  a_scr, b_scr, r_scr, i_scr, first, last):
    tc, w = a_scr.shape
    hd = w // H_A
    x = x_ref[...].astype(F32)
    row = lax.broadcasted_iota(jnp.int32, (tc, w), 0)
    zero_row = jnp.zeros((1, w), F32)
    prev = prev_ref[...].astype(F32)
    nxt = next_ref[...].astype(F32)
    p_last = jnp.where(first, zero_row, prev[15:16, :])
    n0 = jnp.where(last, zero_row, nxt[0:1, :])
    n1 = jnp.where(last, zero_row, nxt[1:2, :])
    x_m1 = jnp.where(row == 0, p_last, pltpu.roll(x, 1, axis=0))
    x_p1 = jnp.where(row == tc - 1, n0, pltpu.roll(x, tc - 1, axis=0))
    x_p2 = jnp.where(row == tc - 2, n0, jnp.where(row == tc - 1, n1, pltpu.roll(x, tc - 2, axis=0)))
    xc = (cw_ref[0:1, :] * x_m1 + cw_ref[1:2, :] * x + cw_ref[2:3, :] * x_p1
          + cw_ref[3:4, :] * x_p2 + cb_ref[...])
    xb = xc.astype(BF16)
    for h in range(H_A):
        sl = slice(h * hd, (h + 1) * hd)
        xh = xb[:, sl]
        r_scr[:, sl] = jnp.dot(xh, wr_ref[h].astype(BF16), preferred_element_type=F32)
        i_scr[:, sl] = jnp.dot(xh, wi_ref[h].astype(BF16), preferred_element_type=F32)
    r = _sigmoid(r_scr[...] + br_ref[...])
    ig = _sigmoid(i_scr[...] + bi_ref[...])
    nl = -lam_ref[...]
    softplus = jnp.maximum(nl, 0.0) + jnp.log1p(jnp.exp(-jnp.abs(nl)))
    log_a = (-LRU_C) * r * softplus
    a_scr[...] = jnp.exp(log_a)
    t = jnp.tanh(log_a)
    one_m = (-2.0 * t) / (1.0 - t)
    b_scr[...] = jnp.sqrt(one_m) * (ig * xc)


def _block_scan(a, b, reverse):
    row = lax.broadcasted_iota(jnp.int32, a.shape, 0)
    for d in (1, 2, 4):
        if reverse:
            keep = row < SUB - d
            shift = SUB - d
        else:
            keep = row >= d
            shift = d
        a_sh = jnp.where(keep, pltpu.roll(a, shift, axis=0), 1.0)
        b_sh = jnp.where(keep, pltpu.roll(b, shift, axis=0), 0.0)
        b = a * b_sh + b
        a = a * a_sh
    return a, b


def _rglru_scan(a_scr, b_scr, h_scr, emit, reverse):
    tc, w = a_scr.shape
    nblk = tc // SUB

    def body(k, h):
        kk = (nblk - 1 - k) if reverse else k
        rows = pl.ds(pl.multiple_of(kk * SUB, SUB), SUB)
        ap, bp = _block_scan(a_scr[rows, :], b_scr[rows, :], reverse)
        out = ap * h + bp
        emit(rows, out)
        edge = out[0:1, :] if reverse else out[SUB - 1:SUB, :]
        return jnp.broadcast_to(edge, (SUB, w))

    h_scr[...] = lax.fori_loop(0, nblk, body, h_scr[...])


def _rglru_fwd_kernel(x_ref, prev_ref, next_ref, cw_ref, cb_ref, wr_ref, br_ref, wi_ref, bi_ref, lam_ref,
                      hf_ref, a_scr, b_scr, r_scr, i_scr, h_scr, *, bpb):
    j = pl.program_id(1)

    @pl.when(j == 0)
    def _():
        h_scr[...] = jnp.zeros_like(h_scr)

    first = jnp.logical_or(j == 0, j == 1)
    last = jnp.logical_or(j == 0, j == bpb)
    _rglru_coeffs(x_ref, prev_ref, next_ref, cw_ref, cb_ref, wr_ref, br_ref, wi_ref, bi_ref, lam_ref,
                  a_scr, b_scr, r_scr, i_scr, first, last)

    def emit(rows, out):
        hf_ref[rows, :] = out

    _rglru_scan(a_scr, b_scr, h_scr, emit, reverse=False)


def _rglru_bwd_kernel(x_ref, prev_ref, next_ref, cw_ref, cb_ref, wr_ref, br_ref, wi_ref, bi_ref, lam_ref,
                      hf_ref, ga_ref, ya_ref, a_scr, b_scr, r_scr, i_scr, h_scr, *, bpb):
    j = pl.program_id(1)

    @pl.when(j == 0)
    def _():
        h_scr[...] = jnp.zeros_like(h_scr)

    first = jnp.logical_or(j == 0, j == bpb)
    last = jnp.logical_or(j == 0, j == 1)
    _rglru_coeffs(x_ref, prev_ref, next_ref, cw_ref, cb_ref, wr_ref, br_ref, wi_ref, bi_ref, lam_ref,
                  a_scr, b_scr, r_scr, i_scr, first, last)

    def emit(rows, out):
        ya_ref[rows, :] = ((hf_ref[rows, :] + out) * ga_ref[rows, :].astype(F32)).astype(ya_ref.dtype)

    _rglru_scan(a_scr, b_scr, h_scr, emit, reverse=True)


def _rglru(xa, ga, p, l, nbatch, bpb):
    rows, w = xa.shape
    hd = w // H_A
    lat_blocks = nbatch * bpb
    ga_blocks = ga.shape[0] // SEG
    nhalo = rows // 16
    per_blk = SEG // 16

    def common_specs(direction, blk):
        def prev_map(b, j):
            return (jnp.maximum(blk(b, j) * per_blk - 1, 0), 0)

        def next_map(b, j):
            return (jnp.minimum((blk(b, j) + 1) * per_blk, nhalo - 1), 0)

        vec = pl.BlockSpec((None, None, 1, w), lambda b, j: (l, direction, 0, 0))
        gw = pl.BlockSpec((None, None, H_A, hd, hd), lambda b, j: (l, direction, 0, 0, 0))
        return [pl.BlockSpec((SEG, w), lambda b, j: (blk(b, j), 0)),
                pl.BlockSpec((16, w), prev_map),
                pl.BlockSpec((16, w), next_map),
                pl.BlockSpec((None, CONV_A, w), lambda b, j: (l, 0, 0)),
                pl.BlockSpec((None, 1, w), lambda b, j: (l, 0, 0)),
                gw, vec, gw, vec, vec]

    def blk_f(b, j):
        return jnp.where(j == 0, lat_blocks + b, b * bpb + j - 1)

    def blk_b(b, j):
        return jnp.where(j == 0, lat_blocks + b, b * bpb + bpb - j)

    scratch = [pltpu.VMEM((SEG, w), F32)] * 4 + [pltpu.VMEM((SUB, w), F32)]
    weights = (p["rg_conv_w"], p["rg_conv_b"], p["rg_w_r"], p["rg_b_r"], p["rg_w_i"], p["rg_b_i"], p["rg_lambda"])
    hf = pl.pallas_call(
        functools.partial(_rglru_fwd_kernel, bpb=bpb),
        grid=(nbatch, bpb + 1),
        in_specs=common_specs(0, blk_f),
        out_specs=pl.BlockSpec((SEG, w), lambda b, j: (blk_f(b, j), 0)),
        out_shape=jax.ShapeDtypeStruct((rows, w), F32),
        scratch_shapes=scratch,
        compiler_params=_cp("arbitrary", "arbitrary"),
        name="rglru_forward",
    )(xa, xa, xa, *weights)
    return pl.pallas_call(
        functools.partial(_rglru_bwd_kernel, bpb=bpb),
        grid=(nbatch, bpb + 1),
        in_specs=common_specs(1, blk_b) + [pl.BlockSpec((SEG, w), lambda b, j: (blk_b(b, j), 0)),
                                           pl.BlockSpec((SEG, w), lambda b, j: (jnp.minimum(blk_b(b, j), ga_blocks - 1), 0))],
        out_specs=pl.BlockSpec((SEG, w), lambda b, j: (blk_b(b, j), 0)),
        out_shape=jax.ShapeDtypeStruct((rows, w), BF16),
        scratch_shapes=scratch,
        compiler_params=_cp("arbitrary", "arbitrary"),
        name="rglru_backward",
    )(xa, xa, xa, *weights, hf, ga)


def _layernorm(x, g, b):
    mu = jnp.mean(x, axis=-1, keepdims=True)
    xc = x - mu
    var = jnp.mean(xc * xc, axis=-1, keepdims=True)
    return xc * lax.rsqrt(var + EPS) * g + b


def _sgu_kernel(u_ref, v_ref, g_ref, b_ref, ws_ref, bs_ref, o_ref):
    tc, w = u_ref.shape
    gw = w // G_B
    v = _layernorm(v_ref[...].astype(F32), g_ref[...], b_ref[...]).astype(BF16)
    for n in range(tc // CHUNK):
        rs = slice(n * CHUNK, (n + 1) * CHUNK)
        for g in range(G_B):
            cs = slice(g * gw, (g + 1) * gw)
            s = jnp.dot(ws_ref[g].astype(BF16), v[rs, cs], preferred_element_type=F32)
            o_ref[rs, cs] = (u_ref[rs, cs].astype(F32) * (s + bs_ref[:, cs])).astype(o_ref.dtype)


def _sgu(uv, p, l):
    rows, w2 = uv.shape
    w = w2 // 2
    depth = p["sg_b"].shape[0]
    bs = jnp.repeat(jnp.swapaxes(p["sg_b"], 1, 2), w // G_B, axis=2)
    vec = pl.BlockSpec((None, 1, w), lambda i: (l, 0, 0))
    return pl.pallas_call(
        _sgu_kernel,
        grid=(rows // SEG,),
        in_specs=[pl.BlockSpec((SEG, w), lambda i: (i, 0)),
                  pl.BlockSpec((SEG, w), lambda i: (i, 1)),
                  vec, vec,
                  pl.BlockSpec((None, G_B, CHUNK, CHUNK), lambda i: (l, 0, 0, 0)),
                  pl.BlockSpec((None, CHUNK, w), lambda i: (l, 0, 0))],
        out_specs=pl.BlockSpec((SEG, w), lambda i: (i, 0)),
        out_shape=jax.ShapeDtypeStruct((rows, w), BF16),
        compiler_params=_cp("arbitrary"),
        name="spatial_gating",
    )(uv, uv, p["sg_ln_g"].reshape(depth, 1, w), p["sg_ln_b"].reshape(depth, 1, w), p["sg_w"], bs)


def _conv_lat_kernel(x_ref, cw_ref, cb_ref, g_ref, b_ref, o_ref, xp, ys):
    nr, wc, c = x_ref.shape
    half = CONV_C // 2
    zeros = jnp.zeros((half, wc, c), F32)
    xp[0:half, :, 0:c] = zeros
    xp[half + nr:half + nr + half, :, 0:c] = zeros
    xp[half:half + nr, :, 0:c] = x_ref[...].astype(F32)

    for g in range(c // LANES):
        ls = slice(g * LANES, (g + 1) * LANES)
        wv = [jnp.broadcast_to(cw_ref[k:k + 1, ls], (SUB, LANES)) for k in range(CONV_C)]

        def taps(r, carry, ls=ls, wv=wv):
            for hs in range(wc // SUB):
                ss = slice(hs * SUB, (hs + 1) * SUB)
                ys[r, ss, ls] = _tree_sum([wv[k] * xp[r + k, ss, ls] for k in range(CONV_C)])
            return carry

        lax.fori_loop(0, nr, taps, 0, unroll=2)

    def finish(r, carry):
        y = _layernorm(ys[r] + cb_ref[...], g_ref[...], b_ref[...])
        o_ref[r] = _silu(y).astype(o_ref.dtype)
        return carry

    lax.fori_loop(0, nr, finish, 0, unroll=4)


def _conv_ctx_kernel(x_ref, cw_ref, cb_ref, g_ref, b_ref, o_ref, xp):
    n, c = x_ref.shape
    half = CONV_C // 2
    pad = 16
    xp[0:pad, :] = jnp.zeros((pad, c), F32)
    xp[pad + n:pad + n + pad, :] = jnp.zeros((pad, c), F32)
    xp[pad:pad + n, :] = x_ref[...].astype(F32)
    acc = cw_ref[0:1, :] * xp[pad - half:pad - half + n, :]
    for k in range(1, CONV_C):
        off = pad - half + k
        acc = acc + cw_ref[k:k + 1, :] * xp[off:off + n, :]
    y = _layernorm(acc + cb_ref[...], g_ref[...], b_ref[...])
    o_ref[...] = _silu(y).astype(o_ref.dtype)


def _conformer(glu, p, l, nbatch, seq, ctx_len, with_ctx):
    rows, c = glu.shape
    depth = p["cv_b"].shape[0]
    nrow = seq // GRID_W
    wt = 16
    cw = p["cv_w"]
    vecs = [p["cv_b"].reshape(depth, 1, c), p["cv_ln_g"].reshape(depth, 1, c), p["cv_ln_b"].reshape(depth, 1, c)]
    g3 = glu.reshape(rows // GRID_W, GRID_W, c)
    half = CONV_C // 2

    lat = pl.pallas_call(
        _conv_lat_kernel,
        grid=(nbatch, GRID_W // wt),
        in_specs=[pl.BlockSpec((nrow, wt, c), lambda b, j: (b, j, 0)),
                  pl.BlockSpec((None, CONV_C, c), lambda b, j: (l, 0, 0))]
                 + [pl.BlockSpec((None, 1, c), lambda b, j: (l, 0, 0))] * 3,
        out_specs=pl.BlockSpec((nrow, wt, c), lambda b, j: (b, j, 0)),
        out_shape=jax.ShapeDtypeStruct((nbatch * nrow, GRID_W, c), BF16),
        scratch_shapes=[pltpu.VMEM((nrow + 2 * half, wt, c + LANES), F32), pltpu.VMEM((nrow, wt, c), F32)],
        compiler_params=_cp("arbitrary", "arbitrary"),
        name="conformer_conv_latent",
    )(g3, cw, *vecs)
    lat = lat.reshape(nbatch * seq, c)
    if not with_ctx:
        return lat
    cblk0 = nbatch * seq // ctx_len
    cx = pl.pallas_call(
        _conv_ctx_kernel,
        grid=(nbatch,),
        in_specs=[pl.BlockSpec((ctx_len, c), lambda b: (cblk0 + b, 0)),
                  pl.BlockSpec((None, CONV_C, c), lambda b: (l, 0, 0))]
                 + [pl.BlockSpec((None, 1, c), lambda b: (l, 0, 0))] * 3,
        out_specs=pl.BlockSpec((ctx_len, c), lambda b: (b, 0)),
        out_shape=jax.ShapeDtypeStruct((nbatch * ctx_len, c), BF16),
        scratch_shapes=[pltpu.VMEM((ctx_len + 32, c), F32)],
        compiler_params=_cp("arbitrary"),
        name="conformer_conv_context",
    )(glu, cw, *vecs)
    return jnp.concatenate([lat, cx], axis=0)


def _ffn_pre_kernel(x_ref, g_ref, sh_ref, sc_ref, wr_ref, hx_ref, aff_ref):
    d = x_ref.shape[1]
    dh = d // 2
    x = x_ref[...]
    y = x * lax.rsqrt(jnp.mean(x * x, axis=-1, keepdims=True) + EPS) * g_ref[...]
    h = y * (1.0 + sc_ref[...]) + sh_ref[...]
    w = wr_ref[...]
    h_hi = h.astype(BF16)
    h_lo = (h - h_hi.astype(F32)).astype(BF16)
    w_hi = w.astype(BF16)
    w_lo = (w - w_hi.astype(F32)).astype(BF16)
    logits = (jnp.dot(h_hi, w_hi, preferred_element_type=F32)
              + (jnp.dot(h_lo, w_hi, preferred_element_type=F32)
                 + jnp.dot(h_hi, w_lo, preferred_element_type=F32)))
    lane = lax.broadcasted_iota(jnp.int32, logits.shape, 1)
    valid = lane < N_EXP
    m = jnp.max(jnp.where(valid, logits, -jnp.inf), axis=-1, keepdims=True)
    ex = jnp.where(valid, jnp.exp(logits - m), 0.0)
    aff = ex / jnp.sum(ex, axis=-1, keepdims=True)
    aff_ref[...] = aff
    lo = pltpu.bitcast(h[:, :dh].astype(BF16).astype(F32), jnp.int32)
    hi = pltpu.bitcast(h[:, dh:].astype(BF16).astype(F32), jnp.int32)
    hx_ref[:, :dh] = jnp.bitwise_or(jnp.bitwise_and(hi, jnp.int32(-65536)),
                                    lax.shift_right_logical(lo, jnp.int32(16)))
    hx_ref[:, dh:] = pltpu.bitcast(aff, jnp.int32)


def _ffn_pre(x, rows, g, modt, wr_pad, l, lat_blocks, bpb):
    d = x.shape[1]
    blk = pl.BlockSpec((SEG, d), lambda i: (i, 0))

    def mod_spec(k):
        return pl.BlockSpec((None, None, 1, d),
                            lambda i: (l, _seg_of_block(i, lat_blocks, bpb) * N_MOD + k, 0, 0))

    return pl.pallas_call(
        _ffn_pre_kernel,
        grid=(rows // SEG,),
        in_specs=[blk,
                  pl.BlockSpec((None, 1, d), lambda i: (l, 0, 0)),
                  mod_spec(3), mod_spec(4),
                  pl.BlockSpec((None, d, LANES), lambda i: (l, 0, 0))],
        out_specs=[pl.BlockSpec((SEG, d // 2 + LANES), lambda i: (i, 0)),
                   pl.BlockSpec((SEG, LANES), lambda i: (i, 0))],
        out_shape=[jax.ShapeDtypeStruct((rows, d // 2 + LANES), jnp.int32),
                   jax.ShapeDtypeStruct((rows, LANES), F32)],
        compiler_params=_cp("arbitrary"),
        name="ffn_pre_router",
    )(x, g, modt, modt, wr_pad)


def _prefix_mats():
    r = lax.broadcasted_iota(jnp.int32, (LANES, LANES), 0)
    c = lax.broadcasted_iota(jnp.int32, (LANES, LANES), 1)
    incl = jnp.where(r <= c, 1.0, 0.0).astype(BF16)
    strict = jnp.where(c < r, 1.0, 0.0).astype(BF16)
    return incl, strict


def _topk_kernel(a_ref, o_ref, thr_scr, *, cap):
    nexp, nc, _ = a_ref.shape
    bits_all = pltpu.bitcast(a_ref[...], jnp.int32)
    thr_all = jnp.zeros((nexp, 1, 1), jnp.int32)
    for bit in range(30, -1, -1):
        cand = thr_all | jnp.int32(1 << bit)
        ge = jnp.where(bits_all >= cand, 1.0, 0.0)
        cnt_all = jnp.sum(jnp.sum(ge, axis=2, keepdims=True), axis=1, keepdims=True)
        thr_all = jnp.where(cnt_all >= float(cap), cand, thr_all)
    thr_scr[...] = jnp.broadcast_to(thr_all, thr_scr.shape)
    incl, strict = _prefix_mats()
    ones_sq = jnp.ones((LANES, LANES), BF16)
    ones8 = jnp.ones((SUB, LANES), BF16)
    lane_c = lax.broadcasted_iota(jnp.int32, (cap, LANES), 1)
    s_col = lax.broadcasted_iota(jnp.int32, (cap, LANES), 0).astype(F32)
    capf = jnp.float32(cap)
    o_ref[...] = jnp.zeros_like(o_ref)

    def total(x):
        return jnp.sum(jnp.sum(x, axis=1, keepdims=True), axis=0, keepdims=True)

    def pad_rows(x):
        if nc == LANES:
            return x
        return jnp.concatenate([x, jnp.zeros((LANES - nc, LANES), x.dtype)], axis=0)

    def prefix(mask_f):
        mp = pad_rows(mask_f).astype(BF16)
        within = jnp.dot(mp, incl, preferred_element_type=F32)
        tot = jnp.dot(mp, ones_sq, preferred_element_type=F32).astype(BF16)
        excl = jnp.dot(strict, tot, preferred_element_type=F32)
        return mp, within, within + excl

    def body(e, carry):
        bits = pltpu.bitcast(a_ref[e], jnp.int32)
        thr = thr_scr[e][0:1, :]
        gt = bits > thr
        eq = bits == thr
        need = capf - total(jnp.where(gt, 1.0, 0.0))
        _, _, pe = prefix(jnp.where(eq, 1.0, 0.0))
        take = jnp.logical_or(gt, jnp.logical_and(eq, pe[:nc] <= need))
        mp, within, _ = prefix(jnp.where(take, 1.0, 0.0))
        s_row = lax.dot_general(ones8, mp, (((1,), (1,)), ((), ())), preferred_element_type=F32)
        pend_row = jnp.dot(s_row.astype(BF16), incl, preferred_element_type=F32)
        pend_b = jnp.broadcast_to(pend_row[0:1, :], (cap, LANES))
        s_b = jnp.broadcast_to(s_row[0:1, :], (cap, LANES))
        before = jnp.logical_and(pend_b <= s_col, lane_c < nc)
        c_s = jnp.sum(jnp.where(before, 1.0, 0.0), axis=1, keepdims=True)
        p_excl = jnp.sum(jnp.where(before, s_b, 0.0), axis=1, keepdims=True)
        sel = jnp.where(lane_c.astype(F32) == c_s, 1.0, 0.0).astype(BF16)
        w_row = jnp.dot(sel, within.astype(BF16), preferred_element_type=F32)
        cnt = jnp.sum(jnp.where(w_row <= s_col - p_excl, 1.0, 0.0), axis=1, keepdims=True)
        idx = (c_s * float(LANES) + cnt).astype(jnp.int32)
        o_ref[...] = jnp.where(lane_c == e, idx, o_ref[...])
        return carry

    lax.fori_loop(0, nexp, body, 0)


def _topk(aff_t, cap):
    nsets, nexp, nc, _ = aff_t.shape
    return pl.pallas_call(
        functools.partial(_topk_kernel, cap=cap),
        grid=(nsets,),
        in_specs=[pl.BlockSpec((None, nexp, nc, LANES), lambda s: (s, 0, 0, 0))],
        out_specs=pl.BlockSpec((None, cap, LANES), lambda s: (s, 0, 0)),
        out_shape=jax.ShapeDtypeStruct((nsets, cap, LANES), jnp.int32),
        scratch_shapes=[pltpu.VMEM((nexp, SUB, LANES), jnp.int32)],
        compiler_params=_cp("arbitrary"),
        name="expert_choice_topk",
    )(aff_t)


def _expert_kernel(idx_ref, hx_hbm, wg_ref, wu_ref, wd_ref, *refs, nf, segs):
    nseg = len(segs)
    gm_refs = refs[:nseg]
    o_ref, xg, xlo, xhi, hid, gate, sem = refs[nseg:]
    e = pl.program_id(0)
    s = pl.program_id(1)
    nr = xg.shape[0] * SUB
    dh = xlo.shape[1]
    tf = wg_ref.shape[1]

    def row_copy(g, u, row, h):
        return pltpu.make_async_copy(hx_hbm.at[pl.ds(row, 1)], xg.at[g, pl.ds(u, 1)], sem.at[h])

    @pl.when(s == 0)
    def _():
        ghalf = nr // SUB // 2
        chalf = nr // ROW_CHUNK // 2

        def issue(h):
            def body(g, c):
                for u in range(SUB):
                    row_copy(g, u, idx_ref[e * nr + g * SUB + u], h).start()
                return c

            lax.fori_loop(h * ghalf, (h + 1) * ghalf, body, 0)

        def drain(h):
            def body(g, c):
                for u in range(SUB):
                    row_copy(g, u, 0, h).wait()
                return c

            lax.fori_loop(h * ghalf, (h + 1) * ghalf, body, 0)

        def unpack(k, c):
            rows = pl.ds(pl.multiple_of(k * ROW_CHUNK, ROW_CHUNK), ROW_CHUNK)
            gchunk = ROW_CHUNK // SUB
            packed = xg[pl.ds(k * gchunk, gchunk)].reshape(ROW_CHUNK, xg.shape[2])
            u = packed[:, :dh]
            xlo[rows, :] = pltpu.bitcast(lax.shift_left(u, jnp.int32(16)), F32).astype(BF16)
            xhi[rows, :] = pltpu.bitcast(jnp.bitwise_and(u, jnp.int32(-65536)), F32).astype(BF16)
            aff = pltpu.bitcast(packed[:, dh:], F32)
            lane = lax.broadcasted_iota(jnp.int32, aff.shape, 1)
            g = jnp.sum(jnp.where(lane == e, aff, 0.0), axis=1, keepdims=True)
            gate[rows, :] = jnp.broadcast_to(g, aff.shape)
            return c

        issue(0)
        issue(1)
        for h in range(2):
            drain(h)
            lax.fori_loop(h * chalf, (h + 1) * chalf, unpack, 0, unroll=2)

    @pl.when(s < nf)
    def _():
        def proj(w_ref):
            return (jnp.dot(xlo[...], w_ref[:dh, :].astype(BF16), preferred_element_type=F32)
                    + jnp.dot(xhi[...], w_ref[dh:, :].astype(BF16), preferred_element_type=F32))

        hid[s] = (_silu(proj(wg_ref)) * proj(wu_ref)).astype(BF16)

    @pl.when(s >= nf)
    def _():
        acc = jnp.dot(hid[0], wd_ref[0:tf, :].astype(BF16), preferred_element_type=F32)
        for f in range(1, nf):
            acc = acc + jnp.dot(hid[f], wd_ref[f * tf:(f + 1) * tf, :].astype(BF16),
                                preferred_element_type=F32)
        for (r0, r1), gm_ref in zip(segs, gm_refs):
            for k in range(o_ref.shape[1] // LANES):
                cs = slice(k * LANES, (k + 1) * LANES)
                o_ref[r0:r1, cs] = (acc[r0:r1, cs] * gate[r0:r1, :] * gm_ref[:, cs]).astype(o_ref.dtype)


def _experts(idx_flat, hx, w_g, w_u, w_d, modt, l, nr, segs, seg_ids):
    d, dexp = w_g.shape[-2:]
    tf = 256 if dexp % 256 == 0 else dexp
    td = 512 if d % 512 == 0 else d
    nf = dexp // tf
    nd = d // td
    wcols = hx.shape[1]

    def gm_spec(seg):
        return pl.BlockSpec((None, None, 1, td),
                            lambda e, s, idx: (l, seg * N_MOD + 5, 0, jnp.maximum(s - nf, 0)))

    grid_spec = pltpu.PrefetchScalarGridSpec(
        num_scalar_prefetch=1,
        grid=(N_EXP, nf + nd),
        in_specs=[pl.BlockSpec(memory_space=pl.ANY),
                  pl.BlockSpec((None, None, d, tf), lambda e, s, idx: (l, e, 0, jnp.minimum(s, nf - 1))),
                  pl.BlockSpec((None, None, d, tf), lambda e, s, idx: (l, e, 0, jnp.minimum(s, nf - 1))),
                  pl.BlockSpec((None, None, dexp, td), lambda e, s, idx: (l, e, 0, jnp.maximum(s - nf, 0)))]
                 + [gm_spec(sg) for sg in seg_ids],
        out_specs=pl.BlockSpec((None, nr, td), lambda e, s, idx: (e, 0, jnp.maximum(s - nf, 0))),
        scratch_shapes=[pltpu.VMEM((nr // SUB, SUB, wcols), jnp.int32),
                        pltpu.VMEM((nr, d // 2), BF16),
                        pltpu.VMEM((nr, d // 2), BF16),
                        pltpu.VMEM((nf, nr, tf), BF16),
                        pltpu.VMEM((nr, LANES), F32),
                        pltpu.SemaphoreType.DMA((2,))],
    )
    return pl.pallas_call(
        functools.partial(_expert_kernel, nf=nf, segs=segs),
        grid_spec=grid_spec,
        out_shape=jax.ShapeDtypeStruct((N_EXP, nr, d), BF16),
        compiler_params=_cp("arbitrary", "arbitrary"),
        name="experts_swiglu",
    )(idx_flat, hx, w_g, w_u, w_d, *([modt] * len(seg_ids)))


def _scatter_kernel(idx_ref, ye_ref, x_in, x_out, buf, rsem, wsem):
    del x_in
    e = pl.program_id(0)
    ngroups, _, d = buf.shape
    nr = ngroups * SUB
    gchunk = ROW_CHUNK // SUB

    def rd(g, u, row, h):
        return pltpu.make_async_copy(x_out.at[pl.ds(row, 1)], buf.at[g, pl.ds(u, 1)], rsem.at[h])

    def wr(g, u, row, h):
        return pltpu.make_async_copy(buf.at[g, pl.ds(u, 1)], x_out.at[pl.ds(row, 1)], wsem.at[h])

    def each(h, fn):
        def body(g, carry):
            for u in range(SUB):
                fn(g, u)
            return carry

        lax.fori_loop(h * (ngroups // 2), (h + 1) * (ngroups // 2), body, 0)

    def add_half(h):
        def add(k, carry):
            rows = pl.ds(pl.multiple_of(k * ROW_CHUNK, ROW_CHUNK), ROW_CHUNK)
            groups = pl.ds(k * gchunk, gchunk)
            buf[groups] = buf[groups] + ye_ref[rows, :].astype(F32).reshape(gchunk, SUB, d)
            return carry

        lax.fori_loop(h * (nr // ROW_CHUNK // 2), (h + 1) * (nr // ROW_CHUNK // 2), add, 0)

    for h in range(2):
        each(h, lambda g, u, h=h: rd(g, u, idx_ref[e * nr + g * SUB + u], h).start())
    for h in range(2):
        each(h, lambda g, u, h=h: rd(g, u, 0, h).wait())
        add_half(h)
        each(h, lambda g, u, h=h: wr(g, u, idx_ref[e * nr + g * SUB + u], h).start())
    for h in range(2):
        each(h, lambda g, u, h=h: wr(g, u, 0, h).wait())


def _scatter_add(idx_flat, ye, x):
    nexp, nr, d = ye.shape
    grid_spec = pltpu.PrefetchScalarGridSpec(
        num_scalar_prefetch=1,
        grid=(nexp,),
        in_specs=[pl.BlockSpec((None, nr, d), lambda e, idx: (e, 0, 0)),
                  pl.BlockSpec(memory_space=pl.ANY)],
        out_specs=pl.BlockSpec(memory_space=pl.ANY),
        scratch_shapes=[pltpu.VMEM((nr // SUB, SUB, d), F32),
                        pltpu.SemaphoreType.DMA((2,)), pltpu.SemaphoreType.DMA((2,))],
    )
    assert nr % (2 * ROW_CHUNK) == 0
    return pl.pallas_call(
        _scatter_kernel,
        grid_spec=grid_spec,
        out_shape=jax.ShapeDtypeStruct(x.shape, x.dtype),
        input_output_aliases={2: 0},
        compiler_params=_cp("arbitrary"),
        name="scatter_add_residual",
    )(idx_flat, ye, x)


def _final_norm_kernel(x_ref, g_ref, o_ref):
    x = x_ref[...]
    o_ref[...] = x * lax.rsqrt(jnp.mean(x * x, axis=-1, keepdims=True) + EPS) * g_ref[...]


def _final_norm(x, g, nrows):
    d = x.shape[1]
    return pl.pallas_call(
        _final_norm_kernel,
        grid=(nrows // SEG,),
        in_specs=[pl.BlockSpec((SEG, d), lambda i: (i, 0)), pl.BlockSpec((1, d), lambda i: (0, 0))],
        out_specs=pl.BlockSpec((SEG, d), lambda i: (i, 0)),
        out_shape=jax.ShapeDtypeStruct((nrows, d), F32),
        compiler_params=_cp("arbitrary"),
        name="final_norm",
    )(x, g.reshape(1, d))


def _route(aff, nbatch, seq, ctx_len, with_ctx):
    lat = nbatch * seq
    cap_l = CAP_FACTOR * seq // N_EXP
    cap_c = CAP_FACTOR * ctx_len // N_EXP
    a_l = aff[:lat, :N_EXP].reshape(nbatch, seq, N_EXP).transpose(0, 2, 1)
    a_l = a_l.reshape(nbatch, N_EXP, seq // LANES, LANES)
    idx_l = _topk(a_l, cap_l)[:, :, :N_EXP]
    idx_l = idx_l + (jnp.arange(nbatch, dtype=jnp.int32) * seq)[:, None, None]
    segs = tuple((b * cap_l, (b + 1) * cap_l) for b in range(nbatch))
    seg_ids = tuple(range(nbatch))
    if not with_ctx:
        idx = idx_l.transpose(2, 0, 1).reshape(N_EXP, nbatch * cap_l)
        return idx.reshape(-1), idx.shape[1], segs, seg_ids
    cpad = SUB * LANES
    a_c = aff[lat:, :N_EXP].reshape(nbatch, ctx_len, N_EXP).transpose(0, 2, 1)
    a_c = jnp.pad(a_c, ((0, 0), (0, 0), (0, cpad - ctx_len)), constant_values=-1.0)
    a_c = a_c.reshape(nbatch, N_EXP, SUB, LANES)
    idx_c = _topk(a_c, cap_c)[:, :, :N_EXP]
    idx_c = idx_c + (lat + jnp.arange(nbatch, dtype=jnp.int32) * ctx_len)[:, None, None]
    idx = jnp.concatenate([idx_l.transpose(2, 0, 1).reshape(N_EXP, nbatch * cap_l),
                           idx_c.transpose(2, 0, 1).reshape(N_EXP, nbatch * cap_c)], axis=1)
    segs = segs + ((nbatch * cap_l, nbatch * (cap_l + cap_c)),)
    seg_ids = seg_ids + (nbatch,)
    return idx.reshape(-1), idx.shape[1], segs, seg_ids


def kernel(x, c, ctx, c_ctx, w_mod, b_mod, g_mix, g_ffn, w_in, b_in, rg_conv_w, rg_conv_b, rg_w_r, rg_b_r, rg_w_i, rg_b_i, rg_lambda, sg_ln_g, sg_ln_b, sg_w, sg_b, cv_w, cv_b, cv_ln_g, cv_ln_b, w_proj_a, w_proj_b, w_proj_c, w_out, w_router, w_e_gate, w_e_up, w_e_down, g_final):
    nbatch, seq, d = x.shape
    ctx_len = ctx.shape[1]
    depth = w_mod.shape[0]
    w_a = rg_conv_w.shape[-1]
    w_b = sg_ln_g.shape[-1]
    w_c = cv_w.shape[-1]
    assert ctx_len == SEG and seq % SEG == 0 and nbatch + 1 <= SUB
    bpb = seq // SEG
    lat_blocks = nbatch * bpb
    lat_rows = nbatch * seq
    all_rows = lat_rows + nbatch * ctx_len
    off_xa = w_a
    off_uv = 2 * w_a
    off_glu = off_uv + 2 * w_b
    off_gate = off_glu + 2 * w_c

    cvec = jnp.concatenate([c, c_ctx[None], jnp.zeros((SUB - nbatch - 1, d), F32)], axis=0)
    modt = _mod_all(cvec, w_mod, b_mod).reshape(depth, SUB * N_MOD, 1, d)

    b_in3 = b_in.reshape(depth, 1, -1)
    gmix3 = g_mix.reshape(depth, 1, d)
    gffn3 = g_ffn.reshape(depth, 1, d)
    wr_pad = jnp.pad(w_router, ((0, 0), (0, 0), (0, LANES - N_EXP)))
    rg = {"rg_conv_w": rg_conv_w, "rg_conv_b": rg_conv_b.reshape(depth, 1, w_a),
          "rg_w_r": rg_w_r, "rg_b_r": rg_b_r.reshape(depth, 2, 1, w_a),
          "rg_w_i": rg_w_i, "rg_b_i": rg_b_i.reshape(depth, 2, 1, w_a),
          "rg_lambda": rg_lambda.reshape(depth, 2, 1, w_a)}
    sg = {"sg_ln_g": sg_ln_g, "sg_ln_b": sg_ln_b, "sg_w": sg_w, "sg_b": sg_b}
    cv = {"cv_w": cv_w, "cv_b": cv_b, "cv_ln_g": cv_ln_g, "cv_ln_b": cv_ln_b}

    for l in range(depth):
        need_ctx = l < depth - 1
        nrows = all_rows if need_ctx else lat_rows
        if l == 0:
            xall, h = _first_norm_mod(x.reshape(lat_rows, d), ctx.reshape(all_rows - lat_rows, d),
                                      gmix3, modt, l, lat_blocks, bpb)
        else:
            h = _norm_mod(xall, gmix3, modt, l, 0, 1, lat_blocks, bpb)
        ga =_mm(h, w_in, b_in3, l, 0, w_a, _gelu, BF16, nrows)
        xa = _mm(h, w_in, b_in3, l, off_xa, w_a, lambda z: z, BF16, all_rows)
        uv = _mm(h, w_in, b_in3, l, off_uv, 2 * w_b, _gelu, BF16, nrows)
        glu = _mm_glu(h, w_in, b_in3, l, off_glu, w_c, nrows)
        gates = _mm(h, w_in, b_in3, l, off_gate, N_BRANCH * d, _sigmoid, BF16, nrows)
        ya = _rglru(xa, ga, rg, l, nbatch, bpb)
        yb = _sgu(uv, sg, l)
        yc = _conformer(glu, cv, l, nbatch, seq, ctx_len, need_ctx)
        y = _merge(ya, yb, yc, w_proj_a, w_proj_b, w_proj_c, gates, l)
        xall = _mm_out_residual(y, w_out, xall, modt, l, seq, lat_rows)
        hx, aff = _ffn_pre(xall, nrows, gffn3, modt, wr_pad, l, lat_blocks, bpb)
        idx_flat, nr, segs, seg_ids = _route(aff, nbatch, seq, ctx_len, need_ctx)
        ye = _experts(idx_flat, hx, w_e_gate, w_e_up, w_e_down, modt, l, nr, segs, seg_ids)
        xall = _scatter_add(idx_flat, ye, xall)
    return _final_norm(xall, g_final, lat_rows).reshape(nbatch, seq, d)
```

```python
import functools

import jax
import jax.numpy as jnp
from jax import lax
from jax.experimental import pallas as pl
from jax.experimental.pallas import tpu as pltpu

GRID_W = 64
N_MOD = 6
H_A = 16
CONV_A = 4
LRU_C = 8.0
G_B = 16
CHUNK = 128
CONV_C = 31
N_BRANCH = 3
N_EXP = 16
CAP_FACTOR = 2
EPS = 1e-6

LANES = 128
SUB = 8
SEG = 256
VMEM_LIMIT = 56 * 1024 * 1024
MM_TN = 256
MM_TM_MAX = 1152
ROW_CHUNK = 32

BF16 = jnp.bfloat16
F32 = jnp.float32


def _cp(*sem):
    return pltpu.CompilerParams(dimension_semantics=sem, vmem_limit_bytes=VMEM_LIMIT)


def _pick_tm(rows, tm_max):
    k = 1
    while rows % k or (rows // k) > tm_max or (rows // k) % 16:
        k += 1
    return rows // k


def _gelu(x):
    return 0.5 * x * (1.0 + jnp.tanh(0.7978845608028654 * (x + 0.044715 * (x * x * x))))


def _sigmoid(x):
    return 1.0 / (1.0 + jnp.exp(-x))


def _silu(x):
    return x * _sigmoid(x)


def _tree_sum(terms):
    while len(terms) > 1:
        terms = [terms[i] + terms[i + 1] for i in range(0, len(terms) - 1, 2)] + (
            [terms[-1]] if len(terms) % 2 else [])
    return terms[0]


def _mod_kernel(a_ref, w_ref, b_ref, o_ref):
    a = _silu(a_ref[...]).astype(BF16)
    o_ref[...] = jnp.dot(a, w_ref[...].astype(BF16), preferred_element_type=F32) + b_ref[...]


def _mod_all(cvec, w_mod, b_mod):
    depth, d, nm = w_mod.shape
    tn = 1024 if nm % 1024 == 0 else nm
    return pl.pallas_call(
        _mod_kernel,
        grid=(depth, nm // tn),
        in_specs=[pl.BlockSpec((SUB, d), lambda l, j: (0, 0)),
                  pl.BlockSpec((None, d, tn), lambda l, j: (l, 0, j)),
                  pl.BlockSpec((None, 1, tn), lambda l, j: (l, 0, j))],
        out_specs=pl.BlockSpec((None, SUB, tn), lambda l, j: (l, 0, j)),
        out_shape=jax.ShapeDtypeStruct((depth, SUB, nm), F32),
        compiler_params=_cp("arbitrary", "arbitrary"),
        name="mod_vectors",
    )(cvec, w_mod, b_mod.reshape(depth, 1, nm))


def _norm_mod_kernel(x_ref, g_ref, sh_ref, sc_ref, o_ref):
    x = x_ref[...]
    y = x * lax.rsqrt(jnp.mean(x * x, axis=-1, keepdims=True) + EPS) * g_ref[...]
    o_ref[...] = (y * (1.0 + sc_ref[...]) + sh_ref[...]).astype(o_ref.dtype)


def _first_norm_mod_kernel(x_ref, c_ref, g_ref, sh_ref, sc_ref, xall_ref, h_ref, *, lat_blocks):
    def emit(src_ref):
        xall_ref[...] = src_ref[...]
        _norm_mod_kernel(src_ref, g_ref, sh_ref, sc_ref, h_ref)

    is_latent = pl.program_id(0) < lat_blocks
    pl.when(is_latent)(lambda: emit(x_ref))
    pl.when(jnp.logical_not(is_latent))(lambda: emit(c_ref))


def _first_norm_mod(x2, c2, g, modt, l, lat_blocks, bpb):
    d = x2.shape[1]
    rows = x2.shape[0] + c2.shape[0]
    cblocks = c2.shape[0] // SEG

    def mod_spec(k):
        return pl.BlockSpec((None, None, 1, d),
                            lambda i: (l, _seg_of_block(i, lat_blocks, bpb) * N_MOD + k, 0, 0))

    blk = pl.BlockSpec((SEG, d), lambda i: (i, 0))
    return pl.pallas_call(
        functools.partial(_first_norm_mod_kernel, lat_blocks=lat_blocks),
        grid=(rows // SEG,),
        in_specs=[pl.BlockSpec((SEG, d), lambda i: (jnp.minimum(i, lat_blocks - 1), 0)),
                  pl.BlockSpec((SEG, d), lambda i: (jnp.clip(i - lat_blocks, 0, cblocks - 1), 0)),
                  pl.BlockSpec((None, 1, d), lambda i: (l, 0, 0)),
                  mod_spec(0), mod_spec(1)],
        out_specs=[blk, blk],
        out_shape=[jax.ShapeDtypeStruct((rows, d), F32), jax.ShapeDtypeStruct((rows, d), BF16)],
        compiler_params=_cp("arbitrary"),
        name="embed_norm_modulate",
    )(x2, c2, g, modt, modt)


def _seg_of_block(i, lat_blocks, blocks_per_batch):
    return jnp.where(i < lat_blocks, i // blocks_per_batch, lat_blocks // blocks_per_batch)


def _norm_mod(x, g, modt, l, k_shift, k_scale, lat_blocks, bpb):
    rows, d = x.shape

    def mod_spec(k):
        return pl.BlockSpec((None, None, 1, d),
                            lambda i: (l, _seg_of_block(i, lat_blocks, bpb) * N_MOD + k, 0, 0))

    return pl.pallas_call(
        _norm_mod_kernel,
        grid=(rows // SEG,),
        in_specs=[pl.BlockSpec((SEG, d), lambda i: (i, 0)),
                  pl.BlockSpec((None, 1, d), lambda i: (l, 0, 0)),
                  mod_spec(k_shift), mod_spec(k_scale)],
        out_specs=pl.BlockSpec((SEG, d), lambda i: (i, 0)),
        out_shape=jax.ShapeDtypeStruct((rows, d), BF16),
        compiler_params=_cp("arbitrary"),
        name="norm_modulate",
    )(x, g, modt, modt)


def _mm_kernel(a_ref, w_ref, b_ref, o_ref, *, epilogue):
    acc = jnp.dot(a_ref[...], w_ref[...].astype(BF16), preferred_element_type=F32) + b_ref[...]
    o_ref[...] = epilogue(acc).astype(o_ref.dtype)


def _mm_glu_kernel(a_ref, wp_ref, wq_ref, bp_ref, bq_ref, o_ref):
    a = a_ref[...]
    p = jnp.dot(a, wp_ref[...].astype(BF16), preferred_element_type=F32) + bp_ref[...]
    q = jnp.dot(a, wq_ref[...].astype(BF16), preferred_element_type=F32) + bq_ref[...]
    o_ref[...] = (p * _sigmoid(q)).astype(o_ref.dtype)


def _mm(a, w, b, l, n0, n_len, epilogue, out_dtype, rows):
    k = a.shape[1]
    tm = _pick_tm(rows, MM_TM_MAX)
    tn = 2 * MM_TN if (n_len % (2 * MM_TN) == 0 and n0 % (2 * MM_TN) == 0) else MM_TN
    j0 = n0 // tn
    return pl.pallas_call(
        functools.partial(_mm_kernel, epilogue=epilogue),
        grid=(rows // tm, n_len // tn),
        in_specs=[pl.BlockSpec((tm, k), lambda i, j: (i, 0)),
                  pl.BlockSpec((None, k, tn), lambda i, j: (l, 0, j0 + j)),
                  pl.BlockSpec((None, 1, tn), lambda i, j: (l, 0, j0 + j))],
        out_specs=pl.BlockSpec((tm, tn), lambda i, j: (i, j)),
        out_shape=jax.ShapeDtypeStruct((rows, n_len), out_dtype),
        compiler_params=_cp("arbitrary", "arbitrary"),
        name="matmul_bias_act",
    )(a, w, b)


def _mm_glu(a, w, b, l, n0, n_half, rows):
    k = a.shape[1]
    tm = _pick_tm(rows, MM_TM_MAX)
    tn = MM_TN
    jp = n0 // tn
    jq = (n0 + n_half) // tn
    return pl.pallas_call(
        _mm_glu_kernel,
        grid=(rows // tm, n_half // tn),
        in_specs=[pl.BlockSpec((tm, k), lambda i, j: (i, 0)),
                  pl.BlockSpec((None, k, tn), lambda i, j: (l, 0, jp + j)),
                  pl.BlockSpec((None, k, tn), lambda i, j: (l, 0, jq + j)),
                  pl.BlockSpec((None, 1, tn), lambda i, j: (l, 0, jp + j)),
                  pl.BlockSpec((None, 1, tn), lambda i, j: (l, 0, jq + j))],
        out_specs=pl.BlockSpec((tm, tn), lambda i, j: (i, j)),
        out_shape=jax.ShapeDtypeStruct((rows, n_half), BF16),
        compiler_params=_cp("arbitrary", "arbitrary"),
        name="matmul_glu",
    )(a, w, w, b, b)


def _row_segment(r, seq, lat_rows):
    return jnp.where(r < lat_rows, r // seq, lat_rows // seq)


def _mm_out_kernel(a_ref, w_ref, x_ref, ma_ref, mb_ref, o_ref, *, seq, lat_rows):
    tm = x_ref.shape[0]
    acc = jnp.dot(a_ref[...], w_ref[...].astype(BF16), preferred_element_type=F32)
    r0 = pl.program_id(0) * tm
    bnd = jnp.where(r0 < lat_rows, (r0 // seq + 1) * seq, r0 + tm) - r0
    row = lax.broadcasted_iota(jnp.int32, acc.shape, 0)
    gate = jnp.where(row < bnd, ma_ref[...], mb_ref[...])
    o_ref[...] = x_ref[...] + gate * acc


def _mm_out_residual(a, w, x, modt, l, seq, lat_rows):
    rows, k = a.shape
    n = w.shape[-1]
    tm = _pick_tm(rows, MM_TM_MAX)
    tn = 2 * MM_TN if n % (2 * MM_TN) == 0 else MM_TN

    def gate_spec(last):
        def index(i, j):
            r = i * tm + (tm - 1 if last else 0)
            return (l, _row_segment(r, seq, lat_rows) * N_MOD + 2, 0, j)

        return pl.BlockSpec((None, None, 1, tn), index)

    return pl.pallas_call(
        functools.partial(_mm_out_kernel, seq=seq, lat_rows=lat_rows),
        grid=(rows // tm, n // tn),
        in_specs=[pl.BlockSpec((tm, k), lambda i, j: (i, 0)),
                  pl.BlockSpec((None, k, tn), lambda i, j: (l, 0, j)),
                  pl.BlockSpec((tm, tn), lambda i, j: (i, j)),
                  gate_spec(False), gate_spec(True)],
        out_specs=pl.BlockSpec((tm, tn), lambda i, j: (i, j)),
        out_shape=jax.ShapeDtypeStruct(x.shape, F32),
        input_output_aliases={2: 0},
        compiler_params=_cp("arbitrary", "arbitrary"),
        name="matmul_out_residual",
    )(a, w, x, modt, modt)


def _merge_kernel(ya_ref, yb_ref, yc_ref, wa_ref, wb_ref, wc_ref, ga_ref, gb_ref, gc_ref, o_ref):
    def branch(y_ref, w_ref, g_ref):
        return g_ref[...].astype(F32) * jnp.dot(y_ref[...], w_ref[...].astype(BF16),
                                                preferred_element_type=F32)

    acc = branch(ya_ref, wa_ref, ga_ref) + branch(yb_ref, wb_ref, gb_ref) + branch(yc_ref, wc_ref, gc_ref)
    o_ref[...] = acc.astype(o_ref.dtype)


def _merge(ya, yb, yc, wpa, wpb, wpc, gates, l):
    rows, k = yb.shape
    d = wpa.shape[-1]
    tm = _pick_tm(rows, MM_TM_MAX)
    tn = MM_TN
    nd = d // tn
    a_spec = pl.BlockSpec((tm, k), lambda i, j: (i, 0))
    w_spec = pl.BlockSpec((None, k, tn), lambda i, j: (l, 0, j))

    def g_spec(br):
        return pl.BlockSpec((tm, tn), lambda i, j: (i, br * nd + j))

    return pl.pallas_call(
        _merge_kernel,
        grid=(rows // tm, nd),
        in_specs=[a_spec, a_spec, a_spec, w_spec, w_spec, w_spec, g_spec(0), g_spec(1), g_spec(2)],
        out_specs=pl.BlockSpec((tm, tn), lambda i, j: (i, j)),
        out_shape=jax.ShapeDtypeStruct((rows, d), BF16),
        compiler_params=_cp("arbitrary", "arbitrary"),
        name="merge_branches",
    )(ya, yb, yc, wpa, wpb, wpc, gates, gates, gates)


def _rglru_coeffs(x_ref, prev_ref, next_ref, cw_ref, cb_ref, wr_ref, br_ref, wi_ref, bi_ref, lam_ref,
                  a_scr, b_scr, r_scr, i_scr, first, last):
    tc, w = a_scr.shape
    hd = w // H_A
    x = x_ref[...].astype(F32)
    row = lax.broadcasted_iota(jnp.int32, (tc, w), 0)
    zero_row = jnp.zeros((1, w), F32)
    prev = prev_ref[...].astype(F32)
    nxt = next_ref[...].astype(F32)
    p_last = jnp.where(first, zero_row, prev[15:16, :])
    n0 = jnp.where(last, zero_row, nxt[0:1, :])
    n1 = jnp.where(last, zero_row, nxt[1:2, :])
    x_m1 = jnp.where(row == 0, p_last, pltpu.roll(x, 1, axis=0))
    x_p1 = jnp.where(row == tc - 1, n0, pltpu.roll(x, tc - 1, axis=0))
    x_p2 = jnp.where(row == tc - 2, n0, jnp.where(row == tc - 1, n1, pltpu.roll(x, tc - 2, axis=0)))
    xc = (cw_ref[0:1, :] * x_m1 + cw_ref[1:2, :] * x + cw_ref[2:3, :] * x_p1
          + cw_ref[3:4, :] * x_p2 + cb_ref[...])
    xb = xc.astype(BF16)
    for h in range(H_A):
        sl = slice(h * hd, (h + 1) * hd)
        xh = xb[:, sl]
        r_scr[:, sl] = jnp.dot(xh, wr_ref[h].astype(BF16), preferred_element_type=F32)
        i_scr[:, sl] = jnp.dot(xh, wi_ref[h].astype(BF16), preferred_element_type=F32)
    r = _sigmoid(r_scr[...] + br_ref[...])
    ig = _sigmoid(i_scr[...] + bi_ref[...])
    nl = -lam_ref[...]
    softplus = jnp.maximum(nl, 0.0) + jnp.log1p(jnp.exp(-jnp.abs(nl)))
    log_a = (-LRU_C) * r * softplus
    a_scr[...] = jnp.exp(log_a)
    t = jnp.tanh(log_a)
    one_m = (-2.0 * t) / (1.0 - t)
    b_scr[...] = jnp.sqrt(one_m) * (ig * xc)


def _block_scan(a, b, reverse):
    row = lax.broadcasted_iota(jnp.int32, a.shape, 0)
    for d in (1, 2, 4):
        if reverse:
            keep = row < SUB - d
            shift = SUB - d
        else:
            keep = row >= d
            shift = d
        a_sh = jnp.where(keep, pltpu.roll(a, shift, axis=0), 1.0)
        b_sh = jnp.where(keep, pltpu.roll(b, shift, axis=0), 0.0)
        b = a * b_sh + b
        a = a * a_sh
    return a, b


def _rglru_scan(a_scr, b_scr, h_scr, emit, reverse):
    tc, w = a_scr.shape
    nblk = tc // SUB

    def body(k, h):
        kk = (nblk - 1 - k) if reverse else k
        rows = pl.ds(pl.multiple_of(kk * SUB, SUB), SUB)
        ap, bp = _block_scan(a_scr[rows, :], b_scr[rows, :], reverse)
        out = ap * h + bp
        emit(rows, out)
        edge = out[0:1, :] if reverse else out[SUB - 1:SUB, :]
        return jnp.broadcast_to(edge, (SUB, w))

    h_scr[...] = lax.fori_loop(0, nblk, body, h_scr[...], unroll=2)


def _rglru_fwd_kernel(x_ref, prev_ref, next_ref, cw_ref, cb_ref, wr_ref, br_ref, wi_ref, bi_ref, lam_ref,
                      hf_ref, a_scr, b_scr, r_scr, i_scr, h_scr, *, bpb):
    j = pl.program_id(1)

    @pl.when(j == 0)
    def _():
        h_scr[...] = jnp.zeros_like(h_scr)

    first = jnp.logical_or(j == 0, j == 1)
    last = jnp.logical_or(j == 0, j == bpb)
    _rglru_coeffs(x_ref, prev_ref, next_ref, cw_ref, cb_ref, wr_ref, br_ref, wi_ref, bi_ref, lam_ref,
                  a_scr, b_scr, r_scr, i_scr, first, last)

    def emit(rows, out):
        hf_ref[rows, :] = out

    _rglru_scan(a_scr, b_scr, h_scr, emit, reverse=False)


def _rglru_bwd_kernel(x_ref, prev_ref, next_ref, cw_ref, cb_ref, wr_ref, br_ref, wi_ref, bi_ref, lam_ref,
                      hf_ref, ga_ref, ya_ref, a_scr, b_scr, r_scr, i_scr, h_scr, *, bpb):
    j = pl.program_id(1)

    @pl.when(j == 0)
    def _():
        h_scr[...] = jnp.zeros_like(h_scr)

    first = jnp.logical_or(j == 0, j == bpb)
    last = jnp.logical_or(j == 0, j == 1)
    _rglru_coeffs(x_ref, prev_ref, next_ref, cw_ref, cb_ref, wr_ref, br_ref, wi_ref, bi_ref, lam_ref,
                  a_scr, b_scr, r_scr, i_scr, first, last)

    def emit(rows, out):
        ya_ref[rows, :] = ((hf_ref[rows, :] + out) * ga_ref[rows, :].astype(F32)).astype(ya_ref.dtype)

    _rglru_scan(a_scr, b_scr, h_scr, emit, reverse=True)


def _rglru(xa, ga, p, l, nbatch, bpb):
    rows, w = xa.shape
    hd = w // H_A
    lat_blocks = nbatch * bpb
    ga_blocks = ga.shape[0] // SEG
    nhalo = rows // 16
    per_blk = SEG // 16

    def common_specs(direction, blk):
        def prev_map(b, j):
            return (jnp.maximum(blk(b, j) * per_blk - 1, 0), 0)

        def next_map(b, j):
            return (jnp.minimum((blk(b, j) + 1) * per_blk, nhalo - 1), 0)

        vec = pl.BlockSpec((None, None, 1, w), lambda b, j: (l, direction, 0, 0))
        gw = pl.BlockSpec((None, None, H_A, hd, hd), lambda b, j: (l, direction, 0, 0, 0))
        return [pl.BlockSpec((SEG, w), lambda b, j: (blk(b, j), 0)),
                pl.BlockSpec((16, w), prev_map),
                pl.BlockSpec((16, w), next_map),
                pl.BlockSpec((None, CONV_A, w), lambda b, j: (l, 0, 0)),
                pl.BlockSpec((None, 1, w), lambda b, j: (l, 0, 0)),
                gw, vec, gw, vec, vec]

    def blk_f(b, j):
        return jnp.where(j == 0, lat_blocks + b, b * bpb + j - 1)

    def blk_b(b, j):
        return jnp.where(j == 0, lat_blocks + b, b * bpb + bpb - j)

    scratch = [pltpu.VMEM((SEG, w), F32)] * 4 + [pltpu.VMEM((SUB, w), F32)]
    weights = (p["rg_conv_w"], p["rg_conv_b"], p["rg_w_r"], p["rg_b_r"], p["rg_w_i"], p["rg_b_i"], p["rg_lambda"])
    hf = pl.pallas_call(
        functools.partial(_rglru_fwd_kernel, bpb=bpb),
        grid=(nbatch, bpb + 1),
        in_specs=common_specs(0, blk_f),
        out_specs=pl.BlockSpec((SEG, w), lambda b, j: (blk_f(b, j), 0)),
        out_shape=jax.ShapeDtypeStruct((rows, w), F32),
        scratch_shapes=scratch,
        compiler_params=_cp("arbitrary", "arbitrary"),
        name="rglru_forward",
    )(xa, xa, xa, *weights)
    return pl.pallas_call(
        functools.partial(_rglru_bwd_kernel, bpb=bpb),
        grid=(nbatch, bpb + 1),
        in_specs=common_specs(1, blk_b) + [pl.BlockSpec((SEG, w), lambda b, j: (blk_b(b, j), 0)),
                                           pl.BlockSpec((SEG, w), lambda b, j: (jnp.minimum(blk_b(b, j), ga_blocks - 1), 0))],
        out_specs=pl.BlockSpec((SEG, w), lambda b, j: (blk_b(b, j), 0)),
        out_shape=jax.ShapeDtypeStruct((rows, w), BF16),
        scratch_shapes=scratch,
        compiler_params=_cp("arbitrary", "arbitrary"),
        name="rglru_backward",
    )(xa, xa, xa, *weights, hf, ga)


def _layernorm(x, g, b):
    mu = jnp.mean(x, axis=-1, keepdims=True)
    xc = x - mu
    var = jnp.mean(xc * xc, axis=-1, keepdims=True)
    return xc * lax.rsqrt(var + EPS) * g + b


def _sgu_kernel(u_ref, v_ref, g_ref, b_ref, ws_ref, bs_ref, o_ref):
    tc, w = u_ref.shape
    gw = w // G_B
    v = _layernorm(v_ref[...].astype(F32), g_ref[...], b_ref[...]).astype(BF16)
    for n in range(tc // CHUNK):
        rs = slice(n * CHUNK, (n + 1) * CHUNK)
        for g in range(G_B):
            cs = slice(g * gw, (g + 1) * gw)
            s = jnp.dot(ws_ref[g].astype(BF16), v[rs, cs], preferred_element_type=F32)
            o_ref[rs, cs] = (u_ref[rs, cs].astype(F32) * (s + bs_ref[:, cs])).astype(o_ref.dtype)


def _sgu(uv, p, l):
    rows, w2 = uv.shape
    w = w2 // 2
    depth = p["sg_b"].shape[0]
    bs = jnp.repeat(jnp.swapaxes(p["sg_b"], 1, 2), w // G_B, axis=2)
    vec = pl.BlockSpec((None, 1, w), lambda i: (l, 0, 0))
    return pl.pallas_call(
        _sgu_kernel,
        grid=(rows // SEG,),
        in_specs=[pl.BlockSpec((SEG, w), lambda i: (i, 0)),
                  pl.BlockSpec((SEG, w), lambda i: (i, 1)),
                  vec, vec,
                  pl.BlockSpec((None, G_B, CHUNK, CHUNK), lambda i: (l, 0, 0, 0)),
                  pl.BlockSpec((None, CHUNK, w), lambda i: (l, 0, 0))],
        out_specs=pl.BlockSpec((SEG, w), lambda i: (i, 0)),
        out_shape=jax.ShapeDtypeStruct((rows, w), BF16),
        compiler_params=_cp("arbitrary"),
        name="spatial_gating",
    )(uv, uv, p["sg_ln_g"].reshape(depth, 1, w), p["sg_ln_b"].reshape(depth, 1, w), p["sg_w"], bs)


def _conv_lat_kernel(x_ref, cw_ref, cb_ref, g_ref, b_ref, o_ref, xp, ys):
    nr, wc, c = x_ref.shape
    half = CONV_C // 2
    zeros = jnp.zeros((half, wc, c), F32)
    xp[0:half, :, 0:c] = zeros
    xp[half + nr:half + nr + half, :, 0:c] = zeros
    xp[half:half + nr, :, 0:c] = x_ref[...].astype(F32)

    for g in range(c // LANES):
        ls = slice(g * LANES, (g + 1) * LANES)
        wv = [jnp.broadcast_to(cw_ref[k:k + 1, ls], (SUB, LANES)) for k in range(CONV_C)]

        def taps(r, carry, ls=ls, wv=wv):
            for hs in range(wc // SUB):
                ss = slice(hs * SUB, (hs + 1) * SUB)
                ys[r, ss, ls] = _tree_sum([wv[k] * xp[r + k, ss, ls] for k in range(CONV_C)])
            return carry

        lax.fori_loop(0, nr, taps, 0, unroll=2)

    def finish(r, carry):
        y = _layernorm(ys[r] + cb_ref[...], g_ref[...], b_ref[...])
        o_ref[r] = _silu(y).astype(o_ref.dtype)
        return carry

    lax.fori_loop(0, nr, finish, 0, unroll=4)


def _conv_ctx_kernel(x_ref, cw_ref, cb_ref, g_ref, b_ref, o_ref, xp):
    n, c = x_ref.shape
    half = CONV_C // 2
    pad = 16
    xp[0:pad, :] = jnp.zeros((pad, c), F32)
    xp[pad + n:pad + n + pad, :] = jnp.zeros((pad, c), F32)
    xp[pad:pad + n, :] = x_ref[...].astype(F32)
    acc = cw_ref[0:1, :] * xp[pad - half:pad - half + n, :]
    for k in range(1, CONV_C):
        off = pad - half + k
        acc = acc + cw_ref[k:k + 1, :] * xp[off:off + n, :]
    y = _layernorm(acc + cb_ref[...], g_ref[...], b_ref[...])
    o_ref[...] = _silu(y).astype(o_ref.dtype)


def _conformer(glu, p, l, nbatch, seq, ctx_len, with_ctx):
    rows, c = glu.shape
    depth = p["cv_b"].shape[0]
    nrow = seq // GRID_W
    wt = 16
    cw = p["cv_w"]
    vecs = [p["cv_b"].reshape(depth, 1, c), p["cv_ln_g"].reshape(depth, 1, c), p["cv_ln_b"].reshape(depth, 1, c)]
    g3 = glu.reshape(rows // GRID_W, GRID_W, c)
    half = CONV_C // 2

    lat = pl.pallas_call(
        _conv_lat_kernel,
        grid=(nbatch, GRID_W // wt),
        in_specs=[pl.BlockSpec((nrow, wt, c), lambda b, j: (b, j, 0)),
                  pl.BlockSpec((None, CONV_C, c), lambda b, j: (l, 0, 0))]
                 + [pl.BlockSpec((None, 1, c), lambda b, j: (l, 0, 0))] * 3,
        out_specs=pl.BlockSpec((nrow, wt, c), lambda b, j: (b, j, 0)),
        out_shape=jax.ShapeDtypeStruct((nbatch * nrow, GRID_W, c), BF16),
        scratch_shapes=[pltpu.VMEM((nrow + 2 * half, wt, c + LANES), F32), pltpu.VMEM((nrow, wt, c), F32)],
        compiler_params=_cp("arbitrary", "arbitrary"),
        name="conformer_conv_latent",
    )(g3, cw, *vecs)
    lat = lat.reshape(nbatch * seq, c)
    if not with_ctx:
        return lat
    cblk0 = nbatch * seq // ctx_len
    cx = pl.pallas_call(
        _conv_ctx_kernel,
        grid=(nbatch,),
        in_specs=[pl.BlockSpec((ctx_len, c), lambda b: (cblk0 + b, 0)),
                  pl.BlockSpec((None, CONV_C, c), lambda b: (l, 0, 0))]
                 + [pl.BlockSpec((None, 1, c), lambda b: (l, 0, 0))] * 3,
        out_specs=pl.BlockSpec((ctx_len, c), lambda b: (b, 0)),
        out_shape=jax.ShapeDtypeStruct((nbatch * ctx_len, c), BF16),
        scratch_shapes=[pltpu.VMEM((ctx_len + 32, c), F32)],
        compiler_params=_cp("arbitrary"),
        name="conformer_conv_context",
    )(glu, cw, *vecs)
    return jnp.concatenate([lat, cx], axis=0)


def _ffn_pre_kernel(x_ref, g_ref, sh_ref, sc_ref, wr_ref, hx_ref, aff_ref):
    d = x_ref.shape[1]
    dh = d // 2
    x = x_ref[...]
    y = x * lax.rsqrt(jnp.mean(x * x, axis=-1, keepdims=True) + EPS) * g_ref[...]
    h = y * (1.0 + sc_ref[...]) + sh_ref[...]
    w = wr_ref[...]
    h_hi = h.astype(BF16)
    h_lo = (h - h_hi.astype(F32)).astype(BF16)
    w_hi = w.astype(BF16)
    w_lo = (w - w_hi.astype(F32)).astype(BF16)
    both = jnp.dot(h_hi, jnp.concatenate([w_hi, w_lo], axis=1), preferred_element_type=F32)
    logits = both[:, :LANES] + (jnp.dot(h_lo, w_hi, preferred_element_type=F32) + both[:, LANES:])
    lane = lax.broadcasted_iota(jnp.int32, logits.shape, 1)
    valid = lane < N_EXP
    m = jnp.max(jnp.where(valid, logits, -jnp.inf), axis=-1, keepdims=True)
    ex = jnp.where(valid, jnp.exp(logits - m), 0.0)
    aff = ex / jnp.sum(ex, axis=-1, keepdims=True)
    aff_ref[...] = aff
    lo = pltpu.bitcast(h[:, :dh].astype(BF16).astype(F32), jnp.int32)
    hi = pltpu.bitcast(h[:, dh:].astype(BF16).astype(F32), jnp.int32)
    hx_ref[:, :dh] = jnp.bitwise_or(jnp.bitwise_and(hi, jnp.int32(-65536)),
                                    lax.shift_right_logical(lo, jnp.int32(16)))
    hx_ref[:, dh:] = pltpu.bitcast(aff, jnp.int32)


def _ffn_pre(x, rows, g, modt, wr_pad, l, lat_blocks, bpb):
    d = x.shape[1]
    blk = pl.BlockSpec((SEG, d), lambda i: (i, 0))

    def mod_spec(k):
        return pl.BlockSpec((None, None, 1, d),
                            lambda i: (l, _seg_of_block(i, lat_blocks, bpb) * N_MOD + k, 0, 0))

    return pl.pallas_call(
        _ffn_pre_kernel,
        grid=(rows // SEG,),
        in_specs=[blk,
                  pl.BlockSpec((None, 1, d), lambda i: (l, 0, 0)),
                  mod_spec(3), mod_spec(4),
                  pl.BlockSpec((None, d, LANES), lambda i: (l, 0, 0))],
        out_specs=[pl.BlockSpec((SEG, d // 2 + LANES), lambda i: (i, 0)),
                   pl.BlockSpec((SEG, LANES), lambda i: (i, 0))],
        out_shape=[jax.ShapeDtypeStruct((rows, d // 2 + LANES), jnp.int32),
                   jax.ShapeDtypeStruct((rows, LANES), F32)],
        compiler_params=_cp("arbitrary"),
        name="ffn_pre_router",
    )(x, g, modt, modt, wr_pad)


def _prefix_mats():
    r = lax.broadcasted_iota(jnp.int32, (LANES, LANES), 0)
    c = lax.broadcasted_iota(jnp.int32, (LANES, LANES), 1)
    incl = jnp.where(r <= c, 1.0, 0.0).astype(BF16)
    strict = jnp.where(c < r, 1.0, 0.0).astype(BF16)
    return incl, strict


def _topk_kernel(a_ref, o_ref, thr_scr, *, cap):
    nexp, nc, _ = a_ref.shape
    bits_all = pltpu.bitcast(a_ref[...], jnp.int32)
    thr_all = jnp.zeros((nexp, 1, 1), jnp.int32)
    for bit in range(30, -1, -1):
        cand = thr_all | jnp.int32(1 << bit)
        ge = jnp.where(bits_all >= cand, 1.0, 0.0)
        cnt_all = jnp.sum(jnp.sum(ge, axis=2, keepdims=True), axis=1, keepdims=True)
        thr_all = jnp.where(cnt_all >= float(cap), cand, thr_all)
    thr_scr[...] = jnp.broadcast_to(thr_all, thr_scr.shape)
    incl, strict = _prefix_mats()
    ones_sq = jnp.ones((LANES, LANES), BF16)
    ones8 = jnp.ones((SUB, LANES), BF16)
    lane_c = lax.broadcasted_iota(jnp.int32, (cap, LANES), 1)
    s_col = lax.broadcasted_iota(jnp.int32, (cap, LANES), 0).astype(F32)
    capf = jnp.float32(cap)
    o_ref[...] = jnp.zeros_like(o_ref)

    def total(x):
        return jnp.sum(jnp.sum(x, axis=1, keepdims=True), axis=0, keepdims=True)

    def pad_rows(x):
        if nc == LANES:
            return x
        return jnp.concatenate([x, jnp.zeros((LANES - nc, LANES), x.dtype)], axis=0)

    def prefix(mask_f):
        mp = pad_rows(mask_f).astype(BF16)
        within = jnp.dot(mp, incl, preferred_element_type=F32)
        tot = jnp.dot(mp, ones_sq, preferred_element_type=F32).astype(BF16)
        excl = jnp.dot(strict, tot, preferred_element_type=F32)
        return mp, within, within + excl

    def body(e, carry):
        bits = pltpu.bitcast(a_ref[e], jnp.int32)
        thr = thr_scr[e][0:1, :]
        gt = bits > thr
        eq = bits == thr
        need = capf - total(jnp.where(gt, 1.0, 0.0))
        _, _, pe = prefix(jnp.where(eq, 1.0, 0.0))
        take = jnp.logical_or(gt, jnp.logical_and(eq, pe[:nc] <= need))
        mp, within, _ = prefix(jnp.where(take, 1.0, 0.0))
        s_row = lax.dot_general(ones8, mp, (((1,), (1,)), ((), ())), preferred_element_type=F32)
        pend_row = jnp.dot(s_row.astype(BF16), incl, preferred_element_type=F32)
        pend_b = jnp.broadcast_to(pend_row[0:1, :], (cap, LANES))
        s_b = jnp.broadcast_to(s_row[0:1, :], (cap, LANES))
        before = jnp.logical_and(pend_b <= s_col, lane_c < nc)
        c_s = jnp.sum(jnp.where(before, 1.0, 0.0), axis=1, keepdims=True)
        p_excl = jnp.sum(jnp.where(before, s_b, 0.0), axis=1, keepdims=True)
        sel = jnp.where(lane_c.astype(F32) == c_s, 1.0, 0.0).astype(BF16)
        w_row = jnp.dot(sel, within.astype(BF16), preferred_element_type=F32)
        cnt = jnp.sum(jnp.where(w_row <= s_col - p_excl, 1.0, 0.0), axis=1, keepdims=True)
        idx = (c_s * float(LANES) + cnt).astype(jnp.int32)
        o_ref[...] = jnp.where(lane_c == e, idx, o_ref[...])
        return carry

    lax.fori_loop(0, nexp, body, 0)


def _topk(aff_t, cap):
    nsets, nexp, nc, _ = aff_t.shape
    return pl.pallas_call(
        functools.partial(_topk_kernel, cap=cap),
        grid=(nsets,),
        in_specs=[pl.BlockSpec((None, nexp, nc, LANES), lambda s: (s, 0, 0, 0))],
        out_specs=pl.BlockSpec((None, cap, LANES), lambda s: (s, 0, 0)),
        out_shape=jax.ShapeDtypeStruct((nsets, cap, LANES), jnp.int32),
        scratch_shapes=[pltpu.VMEM((nexp, SUB, LANES), jnp.int32)],
        compiler_params=_cp("arbitrary"),
        name="expert_choice_topk",
    )(aff_t)


def _expert_kernel(idx_ref, hx_hbm, wg_ref, wu_ref, wd_ref, *refs, nf, segs):
    nseg = len(segs)
    gm_refs = refs[:nseg]
    o_ref, xg, xlo, xhi, hid, gate, sem = refs[nseg:]
    e = pl.program_id(0)
    s = pl.program_id(1)
    nr = xg.shape[0] * SUB
    dh = xlo.shape[1]
    tf = wg_ref.shape[1]

    def row_copy(g, u, row):
        return pltpu.make_async_copy(hx_hbm.at[pl.ds(row, 1)], xg.at[g, pl.ds(u, 1)], sem)

    @pl.when(s == 0)
    def _():
        def issue(g, c):
            for u in range(SUB):
                row_copy(g, u, idx_ref[e * nr + g * SUB + u]).start()
            return c

        lax.fori_loop(0, nr // SUB, issue, 0)

        def drain(g, c):
            for u in range(SUB):
                row_copy(g, u, 0).wait()
            return c

        lax.fori_loop(0, nr // SUB, drain, 0)

        def unpack(k, c):
            rows = pl.ds(pl.multiple_of(k * ROW_CHUNK, ROW_CHUNK), ROW_CHUNK)
            gchunk = ROW_CHUNK // SUB
            packed = xg[pl.ds(k * gchunk, gchunk)].reshape(ROW_CHUNK, xg.shape[2])
            u = packed[:, :dh]
            xlo[rows, :] = pltpu.bitcast(lax.shift_left(u, jnp.int32(16)), F32).astype(BF16)
            xhi[rows, :] = pltpu.bitcast(jnp.bitwise_and(u, jnp.int32(-65536)), F32).astype(BF16)
            aff = pltpu.bitcast(packed[:, dh:], F32)
            lane = lax.broadcasted_iota(jnp.int32, aff.shape, 1)
            g = jnp.sum(jnp.where(lane == e, aff, 0.0), axis=1, keepdims=True)
            gate[rows, :] = jnp.broadcast_to(g, aff.shape)
            return c

        lax.fori_loop(0, nr // ROW_CHUNK, unpack, 0, unroll=2)

    @pl.when(s < nf)
    def _():
        def proj(w_ref):
            return (jnp.dot(xlo[...], w_ref[:dh, :].astype(BF16), preferred_element_type=F32)
                    + jnp.dot(xhi[...], w_ref[dh:, :].astype(BF16), preferred_element_type=F32))

        hid[s] = (_silu(proj(wg_ref)) * proj(wu_ref)).astype(BF16)

    @pl.when(s >= nf)
    def _():
        acc = jnp.dot(hid[0], wd_ref[0:tf, :].astype(BF16), preferred_element_type=F32)
        for f in range(1, nf):
            acc = acc + jnp.dot(hid[f], wd_ref[f * tf:(f + 1) * tf, :].astype(BF16),
                                preferred_element_type=F32)
        for (r0, r1), gm_ref in zip(segs, gm_refs):
            for k in range(o_ref.shape[1] // LANES):
                cs = slice(k * LANES, (k + 1) * LANES)
                o_ref[r0:r1, cs] = (acc[r0:r1, cs] * gate[r0:r1, :] * gm_ref[:, cs]).astype(o_ref.dtype)


def _experts(idx_flat, hx, w_g, w_u, w_d, modt, l, nr, segs, seg_ids):
    d, dexp = w_g.shape[-2:]
    tf = 256 if dexp % 256 == 0 else dexp
    td = 1024 if d % 1024 == 0 else d
    nf = dexp // tf
    nd = d // td
    wcols = hx.shape[1]

    def gm_spec(seg):
        return pl.BlockSpec((None, None, 1, td),
                            lambda e, s, idx: (l, seg * N_MOD + 5, 0, jnp.maximum(s - nf, 0)))

    grid_spec = pltpu.PrefetchScalarGridSpec(
        num_scalar_prefetch=1,
        grid=(N_EXP, nf + nd),
        in_specs=[pl.BlockSpec(memory_space=pl.ANY),
                  pl.BlockSpec((None, None, d, tf), lambda e, s, idx: (l, e, 0, jnp.minimum(s, nf - 1))),
                  pl.BlockSpec((None, None, d, tf), lambda e, s, idx: (l, e, 0, jnp.minimum(s, nf - 1))),
                  pl.BlockSpec((None, None, dexp, td), lambda e, s, idx: (l, e, 0, jnp.maximum(s - nf, 0)))]
                 + [gm_spec(sg) for sg in seg_ids],
        out_specs=pl.BlockSpec((None, nr, td), lambda e, s, idx: (e, 0, jnp.maximum(s - nf, 0))),
        scratch_shapes=[pltpu.VMEM((nr // SUB, SUB, wcols), jnp.int32),
                        pltpu.VMEM((nr, d // 2), BF16),
                        pltpu.VMEM((nr, d // 2), BF16),
                        pltpu.VMEM((nf, nr, tf), BF16),
                        pltpu.VMEM((nr, LANES), F32),
                        pltpu.SemaphoreType.DMA(())],
    )
    return pl.pallas_call(
        functools.partial(_expert_kernel, nf=nf, segs=segs),
        grid_spec=grid_spec,
        out_shape=jax.ShapeDtypeStruct((N_EXP, nr, d), BF16),
        compiler_params=_cp("arbitrary", "arbitrary"),
        name="experts_swiglu",
    )(idx_flat, hx, w_g, w_u, w_d, *([modt] * len(seg_ids)))


def _scatter_kernel(idx_ref, ye_ref, x_in, x_out, buf, rsem, wsem):
    del x_in
    e = pl.program_id(0)
    ngroups, _, d = buf.shape
    nr = ngroups * SUB
    gchunk = ROW_CHUNK // SUB

    def rd(g, u, row, h):
        return pltpu.make_async_copy(x_out.at[pl.ds(row, 1)], buf.at[g, pl.ds(u, 1)], rsem.at[h])

    def wr(g, u, row, h):
        return pltpu.make_async_copy(buf.at[g, pl.ds(u, 1)], x_out.at[pl.ds(row, 1)], wsem.at[h])

    def each(h, fn):
        def body(g, carry):
            for u in range(SUB):
                fn(g, u)
            return carry

        lax.fori_loop(h * (ngroups // 2), (h + 1) * (ngroups // 2), body, 0)

    def add_half(h):
        def add(k, carry):
            rows = pl.ds(pl.multiple_of(k * ROW_CHUNK, ROW_CHUNK), ROW_CHUNK)
            groups = pl.ds(k * gchunk, gchunk)
            buf[groups] = buf[groups] + ye_ref[rows, :].astype(F32).reshape(gchunk, SUB, d)
            return carry

        lax.fori_loop(h * (nr // ROW_CHUNK // 2), (h + 1) * (nr // ROW_CHUNK // 2), add, 0)

    for h in range(2):
        each(h, lambda g, u, h=h: rd(g, u, idx_ref[e * nr + g * SUB + u], h).start())
    for h in range(2):
        each(h, lambda g, u, h=h: rd(g, u, 0, h).wait())
        add_half(h)
        each(h, lambda g, u, h=h: wr(g, u, idx_ref[e * nr + g * SUB + u], h).start())
    for h in range(2):
        each(h, lambda g, u, h=h: wr(g, u, 0, h).wait())


def _scatter_add(idx_flat, ye, x):
    nexp, nr, d = ye.shape
    grid_spec = pltpu.PrefetchScalarGridSpec(
        num_scalar_prefetch=1,
        grid=(nexp,),
        in_specs=[pl.BlockSpec((None, nr, d), lambda e, idx: (e, 0, 0)),
                  pl.BlockSpec(memory_space=pl.ANY)],
        out_specs=pl.BlockSpec(memory_space=pl.ANY),
        scratch_shapes=[pltpu.VMEM((nr // SUB, SUB, d), F32),
                        pltpu.SemaphoreType.DMA((2,)), pltpu.SemaphoreType.DMA((2,))],
    )
    assert nr % (2 * ROW_CHUNK) == 0
    return pl.pallas_call(
        _scatter_kernel,
        grid_spec=grid_spec,
        out_shape=jax.ShapeDtypeStruct(x.shape, x.dtype),
        input_output_aliases={2: 0},
        compiler_params=_cp("arbitrary"),
        name="scatter_add_residual",
    )(idx_flat, ye, x)


def _final_norm_kernel(x_ref, g_ref, o_ref):
    x = x_ref[...]
    o_ref[...] = x * lax.rsqrt(jnp.mean(x * x, axis=-1, keepdims=True) + EPS) * g_ref[...]


def _final_norm(x, g, nrows):
    d = x.shape[1]
    return pl.pallas_call(
        _final_norm_kernel,
        grid=(nrows // SEG,),
        in_specs=[pl.BlockSpec((SEG, d), lambda i: (i, 0)), pl.BlockSpec((1, d), lambda i: (0, 0))],
        out_specs=pl.BlockSpec((SEG, d), lambda i: (i, 0)),
        out_shape=jax.ShapeDtypeStruct((nrows, d), F32),
        compiler_params=_cp("arbitrary"),
        name="final_norm",
    )(x, g.reshape(1, d))


def _route(aff, nbatch, seq, ctx_len, with_ctx):
    lat = nbatch * seq
    cap_l = CAP_FACTOR * seq // N_EXP
    cap_c = CAP_FACTOR * ctx_len // N_EXP
    a_l = aff[:lat, :N_EXP].reshape(nbatch, seq, N_EXP).transpose(0, 2, 1)
    a_l = a_l.reshape(nbatch, N_EXP, seq // LANES, LANES)
    idx_l = _topk(a_l, cap_l)[:, :, :N_EXP]
    idx_l = idx_l + (jnp.arange(nbatch, dtype=jnp.int32) * seq)[:, None, None]
    segs = tuple((b * cap_l, (b + 1) * cap_l) for b in range(nbatch))
    seg_ids = tuple(range(nbatch))
    if not with_ctx:
        idx = idx_l.transpose(2, 0, 1).reshape(N_EXP, nbatch * cap_l)
        return idx.reshape(-1), idx.shape[1], segs, seg_ids
    cpad = SUB * LANES
    a_c = aff[lat:, :N_EXP].reshape(nbatch, ctx_len, N_EXP).transpose(0, 2, 1)
    a_c = jnp.pad(a_c, ((0, 0), (0, 0), (0, cpad - ctx_len)), constant_values=-1.0)
    a_c = a_c.reshape(nbatch, N_EXP, SUB, LANES)
    idx_c = _topk(a_c, cap_c)[:, :, :N_EXP]
    idx_c = idx_c + (lat + jnp.arange(nbatch, dtype=jnp.int32) * ctx_len)[:, None, None]
    idx = jnp.concatenate([idx_l.transpose(2, 0, 1).reshape(N_EXP, nbatch * cap_l),
                           idx_c.transpose(2, 0, 1).reshape(N_EXP, nbatch * cap_c)], axis=1)
    segs = segs + ((nbatch * cap_l, nbatch * (cap_l + cap_c)),)
    seg_ids = seg_ids + (nbatch,)
    return idx.reshape(-1), idx.shape[1], segs, seg_ids


def kernel(x, c, ctx, c_ctx, w_mod, b_mod, g_mix, g_ffn, w_in, b_in, rg_conv_w, rg_conv_b, rg_w_r, rg_b_r, rg_w_i, rg_b_i, rg_lambda, sg_ln_g, sg_ln_b, sg_w, sg_b, cv_w, cv_b, cv_ln_g, cv_ln_b, w_proj_a, w_proj_b, w_proj_c, w_out, w_router, w_e_gate, w_e_up, w_e_down, g_final):
    nbatch, seq, d = x.shape
    ctx_len = ctx.shape[1]
    depth = w_mod.shape[0]
    w_a = rg_conv_w.shape[-1]
    w_b = sg_ln_g.shape[-1]
    w_c = cv_w.shape[-1]
    assert ctx_len == SEG and seq % SEG == 0 and nbatch + 1 <= SUB
    bpb = seq // SEG
    lat_blocks = nbatch * bpb
    lat_rows = nbatch * seq
    all_rows = lat_rows + nbatch * ctx_len
    off_xa = w_a
    off_uv = 2 * w_a
    off_glu = off_uv + 2 * w_b
    off_gate = off_glu + 2 * w_c

    cvec = jnp.concatenate([c, c_ctx[None], jnp.zeros((SUB - nbatch - 1, d), F32)], axis=0)
    modt = _mod_all(cvec, w_mod, b_mod).reshape(depth, SUB * N_MOD, 1, d)

    b_in3 = b_in.reshape(depth, 1, -1)
    gmix3 = g_mix.reshape(depth, 1, d)
    gffn3 = g_ffn.reshape(depth, 1, d)
    wr_pad = jnp.pad(w_router, ((0, 0), (0, 0), (0, LANES - N_EXP)))
    rg = {"rg_conv_w": rg_conv_w, "rg_conv_b": rg_conv_b.reshape(depth, 1, w_a),
          "rg_w_r": rg_w_r, "rg_b_r": rg_b_r.reshape(depth, 2, 1, w_a),
          "rg_w_i": rg_w_i, "rg_b_i": rg_b_i.reshape(depth, 2, 1, w_a),
          "rg_lambda": rg_lambda.reshape(depth, 2, 1, w_a)}
    sg = {"sg_ln_g": sg_ln_g, "sg_ln_b": sg_ln_b, "sg_w": sg_w, "sg_b": sg_b}
    cv = {"cv_w": cv_w, "cv_b": cv_b, "cv_ln_g": cv_ln_g, "cv_ln_b": cv_ln_b}

    for l in range(depth):
        need_ctx = l < depth - 1
        nrows = all_rows if need_ctx else lat_rows
        if l == 0:
            xall, h = _first_norm_mod(x.reshape(lat_rows, d), ctx.reshape(all_rows - lat_rows, d),
                                      gmix3, modt, l, lat_blocks, bpb)
        else:
            h = _norm_mod(xall, gmix3, modt, l, 0, 1, lat_blocks, bpb)
        ga =_mm(h, w_in, b_in3, l, 0, w_a, _gelu, BF16, nrows)
        xa = _mm(h, w_in, b_in3, l, off_xa, w_a, lambda z: z, BF16, all_rows)
        uv = _mm(h, w_in, b_in3, l, off_uv, 2 * w_b, _gelu, BF16, nrows)
        glu = _mm_glu(h, w_in, b_in3, l, off_glu, w_c, nrows)
        gates = _mm(h, w_in, b_in3, l, off_gate, N_BRANCH * d, _sigmoid, BF16, nrows)
        ya = _rglru(xa, ga, rg, l, nbatch, bpb)
        yb = _sgu(uv, sg, l)
        yc = _conformer(glu, cv, l, nbatch, seq, ctx_len, need_ctx)
        y = _merge(ya, yb, yc, w_proj_a, w_proj_b, w_proj_c, gates, l)
        xall = _mm_out_residual(y, w_out, xall, modt, l, seq, lat_rows)
        hx, aff = _ffn_pre(xall, nrows, gffn3, modt, wr_pad, l, lat_blocks, bpb)
        idx_flat, nr, segs, seg_ids = _route(aff, nbatch, seq, ctx_len, need_ctx)
        ye = _experts(idx_flat, hx, w_e_gate, w_e_up, w_e_down, modt, l, nr, segs, seg_ids)
        xall = _scatter_add(idx_flat, ye, xall)
    return _final_norm(xall, g_final, lat_rows).reshape(nbatch, seq, d)
```

```python
import functools

import jax
import jax.numpy as jnp
from jax import lax
from jax.experimental import pallas as pl
from jax.experimental.pallas import tpu as pltpu

GRID_W = 64
N_MOD = 6
H_A = 16
CONV_A = 4
LRU_C = 8.0
G_B = 16
CHUNK = 128
CONV_C = 31
N_BRANCH = 3
N_EXP = 16
CAP_FACTOR = 2
EPS = 1e-6

LANES = 128
SUB = 8
SEG = 256
VMEM_LIMIT = 56 * 1024 * 1024
MM_TN = 256
MM_TM_MAX = 1152
ROW_CHUNK = 32

BF16 = jnp.bfloat16
F32 = jnp.float32


def _cp(*sem):
    return pltpu.CompilerParams(dimension_semantics=sem, vmem_limit_bytes=VMEM_LIMIT)


def _pick_tm(rows, tm_max):
    k = 1
    while rows % k or (rows // k) > tm_max or (rows // k) % 16:
        k += 1
    return rows // k


def _gelu(x):
    return 0.5 * x * (1.0 + jnp.tanh(0.7978845608028654 * (x + 0.044715 * (x * x * x))))


def _sigmoid(x):
    return 1.0 / (1.0 + jnp.exp(-x))


def _silu(x):
    return x * _sigmoid(x)


def _tree_sum(terms):
    while len(terms) > 1:
        terms = [terms[i] + terms[i + 1] for i in range(0, len(terms) - 1, 2)] + (
            [terms[-1]] if len(terms) % 2 else [])
    return terms[0]


def _mod_kernel(a_ref, w_ref, b_ref, o_ref):
    a = _silu(a_ref[...]).astype(BF16)
    o_ref[...] = jnp.dot(a, w_ref[...].astype(BF16), preferred_element_type=F32) + b_ref[...]


def _mod_all(cvec, w_mod, b_mod):
    depth, d, nm = w_mod.shape
    tn = 1024 if nm % 1024 == 0 else nm
    return pl.pallas_call(
        _mod_kernel,
        grid=(depth, nm // tn),
        in_specs=[pl.BlockSpec((SUB, d), lambda l, j: (0, 0)),
                  pl.BlockSpec((None, d, tn), lambda l, j: (l, 0, j)),
                  pl.BlockSpec((None, 1, tn), lambda l, j: (l, 0, j))],
        out_specs=pl.BlockSpec((None, SUB, tn), lambda l, j: (l, 0, j)),
        out_shape=jax.ShapeDtypeStruct((depth, SUB, nm), F32),
        compiler_params=_cp("arbitrary", "arbitrary"),
        name="mod_vectors",
    )(cvec, w_mod, b_mod.reshape(depth, 1, nm))


def _norm_mod_kernel(x_ref, g_ref, sh_ref, sc_ref, o_ref):
    x = x_ref[...]
    y = x * lax.rsqrt(jnp.mean(x * x, axis=-1, keepdims=True) + EPS) * g_ref[...]
    o_ref[...] = (y * (1.0 + sc_ref[...]) + sh_ref[...]).astype(o_ref.dtype)


def _first_norm_mod_kernel(x_ref, c_ref, g_ref, sh_ref, sc_ref, xall_ref, h_ref, *, lat_blocks):
    def emit(src_ref):
        xall_ref[...] = src_ref[...]
        _norm_mod_kernel(src_ref, g_ref, sh_ref, sc_ref, h_ref)

    is_latent = pl.program_id(0) < lat_blocks
    pl.when(is_latent)(lambda: emit(x_ref))
    pl.when(jnp.logical_not(is_latent))(lambda: emit(c_ref))


def _first_norm_mod(x2, c2, g, modt, l, lat_blocks, bpb):
    d = x2.shape[1]
    rows = x2.shape[0] + c2.shape[0]
    cblocks = c2.shape[0] // SEG

    def mod_spec(k):
        return pl.BlockSpec((None, None, 1, d),
                            lambda i: (l, _seg_of_block(i, lat_blocks, bpb) * N_MOD + k, 0, 0))

    blk = pl.BlockSpec((SEG, d), lambda i: (i, 0))
    return pl.pallas_call(
        functools.partial(_first_norm_mod_kernel, lat_blocks=lat_blocks),
        grid=(rows // SEG,),
        in_specs=[pl.BlockSpec((SEG, d), lambda i: (jnp.minimum(i, lat_blocks - 1), 0)),
                  pl.BlockSpec((SEG, d), lambda i: (jnp.clip(i - lat_blocks, 0, cblocks - 1), 0)),
                  pl.BlockSpec((None, 1, d), lambda i: (l, 0, 0)),
                  mod_spec(0), mod_spec(1)],
        out_specs=[blk, blk],
        out_shape=[jax.ShapeDtypeStruct((rows, d), F32), jax.ShapeDtypeStruct((rows, d), BF16)],
        compiler_params=_cp("arbitrary"),
        name="embed_norm_modulate",
    )(x2, c2, g, modt, modt)


def _seg_of_block(i, lat_blocks, blocks_per_batch):
    return jnp.where(i < lat_blocks, i // blocks_per_batch, lat_blocks // blocks_per_batch)


def _norm_mod(x, g, modt, l, k_shift, k_scale, lat_blocks, bpb):
    rows, d = x.shape

    def mod_spec(k):
        return pl.BlockSpec((None, None, 1, d),
                            lambda i: (l, _seg_of_block(i, lat_blocks, bpb) * N_MOD + k, 0, 0))

    return pl.pallas_call(
        _norm_mod_kernel,
        grid=(rows // SEG,),
        in_specs=[pl.BlockSpec((SEG, d), lambda i: (i, 0)),
                  pl.BlockSpec((None, 1, d), lambda i: (l, 0, 0)),
                  mod_spec(k_shift), mod_spec(k_scale)],
        out_specs=pl.BlockSpec((SEG, d), lambda i: (i, 0)),
        out_shape=jax.ShapeDtypeStruct((rows, d), BF16),
        compiler_params=_cp("arbitrary"),
        name="norm_modulate",
    )(x, g, modt, modt)


def _mm_kernel(a_ref, w_ref, b_ref, o_ref, *, epilogue):
    acc = jnp.dot(a_ref[...], w_ref[...].astype(BF16), preferred_element_type=F32) + b_ref[...]
    o_ref[...] = epilogue(acc).astype(o_ref.dtype)


def _mm_glu_kernel(a_ref, wp_ref, wq_ref, bp_ref, bq_ref, o_ref):
    a = a_ref[...]
    p = jnp.dot(a, wp_ref[...].astype(BF16), preferred_element_type=F32) + bp_ref[...]
    q = jnp.dot(a, wq_ref[...].astype(BF16), preferred_element_type=F32) + bq_ref[...]
    o_ref[...] = (p * _sigmoid(q)).astype(o_ref.dtype)


def _mm(a, w, b, l, n0, n_len, epilogue, out_dtype, rows):
    k = a.shape[1]
    tm = _pick_tm(rows, MM_TM_MAX)
    tn = 2 * MM_TN if (n_len % (2 * MM_TN) == 0 and n0 % (2 * MM_TN) == 0) else MM_TN
    j0 = n0 // tn
    return pl.pallas_call(
        functools.partial(_mm_kernel, epilogue=epilogue),
        grid=(rows // tm, n_len // tn),
        in_specs=[pl.BlockSpec((tm, k), lambda i, j: (i, 0)),
                  pl.BlockSpec((None, k, tn), lambda i, j: (l, 0, j0 + j)),
                  pl.BlockSpec((None, 1, tn), lambda i, j: (l, 0, j0 + j))],
        out_specs=pl.BlockSpec((tm, tn), lambda i, j: (i, j)),
        out_shape=jax.ShapeDtypeStruct((rows, n_len), out_dtype),
        compiler_params=_cp("arbitrary", "arbitrary"),
        name="matmul_bias_act",
    )(a, w, b)


def _mm_glu(a, w, b, l, n0, n_half, rows):
    k = a.shape[1]
    tm = _pick_tm(rows, MM_TM_MAX)
    tn = MM_TN
    jp = n0 // tn
    jq = (n0 + n_half) // tn
    return pl.pallas_call(
        _mm_glu_kernel,
        grid=(rows // tm, n_half // tn),
        in_specs=[pl.BlockSpec((tm, k), lambda i, j: (i, 0)),
                  pl.BlockSpec((None, k, tn), lambda i, j: (l, 0, jp + j)),
                  pl.BlockSpec((None, k, tn), lambda i, j: (l, 0, jq + j)),
                  pl.BlockSpec((None, 1, tn), lambda i, j: (l, 0, jp + j)),
                  pl.BlockSpec((None, 1, tn), lambda i, j: (l, 0, jq + j))],
        out_specs=pl.BlockSpec((tm, tn), lambda i, j: (i, j)),
        out_shape=jax.ShapeDtypeStruct((rows, n_half), BF16),
        compiler_params=_cp("arbitrary", "arbitrary"),
        name="matmul_glu",
    )(a, w, w, b, b)


def _row_segment(r, seq, lat_rows):
    return jnp.where(r < lat_rows, r // seq, lat_rows // seq)


def _mm_out_kernel(a_ref, w_ref, x_ref, ma_ref, mb_ref, o_ref, *, seq, lat_rows):
    tm = x_ref.shape[0]
    acc = jnp.dot(a_ref[...], w_ref[...].astype(BF16), preferred_element_type=F32)
    r0 = pl.program_id(0) * tm
    bnd = jnp.where(r0 < lat_rows, (r0 // seq + 1) * seq, r0 + tm) - r0
    row = lax.broadcasted_iota(jnp.int32, acc.shape, 0)
    gate = jnp.where(row < bnd, ma_ref[...], mb_ref[...])
    o_ref[...] = x_ref[...] + gate * acc


def _mm_out_residual(a, w, x, modt, l, seq, lat_rows):
    rows, k = a.shape
    n = w.shape[-1]
    tm = _pick_tm(rows, MM_TM_MAX)
    tn = 2 * MM_TN if n % (2 * MM_TN) == 0 else MM_TN

    def gate_spec(last):
        def index(i, j):
            r = i * tm + (tm - 1 if last else 0)
            return (l, _row_segment(r, seq, lat_rows) * N_MOD + 2, 0, j)

        return pl.BlockSpec((None, None, 1, tn), index)

    return pl.pallas_call(
        functools.partial(_mm_out_kernel, seq=seq, lat_rows=lat_rows),
        grid=(rows // tm, n // tn),
        in_specs=[pl.BlockSpec((tm, k), lambda i, j: (i, 0)),
                  pl.BlockSpec((None, k, tn), lambda i, j: (l, 0, j)),
                  pl.BlockSpec((tm, tn), lambda i, j: (i, j)),
                  gate_spec(False), gate_spec(True)],
        out_specs=pl.BlockSpec((tm, tn), lambda i, j: (i, j)),
        out_shape=jax.ShapeDtypeStruct(x.shape, F32),
        input_output_aliases={2: 0},
        compiler_params=_cp("arbitrary", "arbitrary"),
        name="matmul_out_residual",
    )(a, w, x, modt, modt)


def _merge_kernel(ya_ref, yb_ref, yc_ref, wa_ref, wb_ref, wc_ref, ga_ref, gb_ref, gc_ref, o_ref):
    def branch(y_ref, w_ref, g_ref):
        return g_ref[...].astype(F32) * jnp.dot(y_ref[...], w_ref[...].astype(BF16),
                                                preferred_element_type=F32)

    acc = branch(ya_ref, wa_ref, ga_ref) + branch(yb_ref, wb_ref, gb_ref) + branch(yc_ref, wc_ref, gc_ref)
    o_ref[...] = acc.astype(o_ref.dtype)


def _merge(ya, yb, yc, wpa, wpb, wpc, gates, l):
    rows, k = yb.shape
    d = wpa.shape[-1]
    tm = _pick_tm(rows, MM_TM_MAX)
    tn = MM_TN
    nd = d // tn
    a_spec = pl.BlockSpec((tm, k), lambda i, j: (i, 0))
    w_spec = pl.BlockSpec((None, k, tn), lambda i, j: (l, 0, j))

    def g_spec(br):
        return pl.BlockSpec((tm, tn), lambda i, j: (i, br * nd + j))

    return pl.pallas_call(
        _merge_kernel,
        grid=(rows // tm, nd),
        in_specs=[a_spec, a_spec, a_spec, w_spec, w_spec, w_spec, g_spec(0), g_spec(1), g_spec(2)],
        out_specs=pl.BlockSpec((tm, tn), lambda i, j: (i, j)),
        out_shape=jax.ShapeDtypeStruct((rows, d), BF16),
        compiler_params=_cp("arbitrary", "arbitrary"),
        name="merge_branches",
    )(ya, yb, yc, wpa, wpb, wpc, gates, gates, gates)


def _rglru_coeffs(x_ref, prev_ref, next_ref, cw_ref, cb_ref, wr_ref, br_ref, wi_ref, bi_ref, lam_ref,
                  a_scr, b_scr, r_scr, i_scr, first, last):
    tc, w = a_scr.shape
    hd = w // H_A
    x = x_ref[...].astype(F32)
    row = lax.broadcasted_iota(jnp.int32, (tc, w), 0)
    zero_row = jnp.zeros((1, w), F32)
    prev = prev_ref[...].astype(F32)
    nxt = next_ref[...].astype(F32)
    p_last = jnp.where(first, zero_row, prev[15:16, :])
    n0 = jnp.where(last, zero_row, nxt[0:1, :])
    n1 = jnp.where(last, zero_row, nxt[1:2, :])
    x_m1 = jnp.where(row == 0, p_last, pltpu.roll(x, 1, axis=0))
    x_p1 = jnp.where(row == tc - 1, n0, pltpu.roll(x, tc - 1, axis=0))
    x_p2 = jnp.where(row == tc - 2, n0, jnp.where(row == tc - 1, n1, pltpu.roll(x, tc - 2, axis=0)))
    xc = (cw_ref[0:1, :] * x_m1 + cw_ref[1:2, :] * x + cw_ref[2:3, :] * x_p1
          + cw_ref[3:4, :] * x_p2 + cb_ref[...])
    xb = xc.astype(BF16)
    for h in range(H_A):
        sl = slice(h * hd, (h + 1) * hd)
        xh = xb[:, sl]
        r_scr[:, sl] = jnp.dot(xh, wr_ref[h].astype(BF16), preferred_element_type=F32)
        i_scr[:, sl] = jnp.dot(xh, wi_ref[h].astype(BF16), preferred_element_type=F32)
    r = _sigmoid(r_scr[...] + br_ref[...])
    ig = _sigmoid(i_scr[...] + bi_ref[...])
    nl = -lam_ref[...]
    softplus = jnp.maximum(nl, 0.0) + jnp.log1p(jnp.exp(-jnp.abs(nl)))
    log_a = (-LRU_C) * r * softplus
    a_scr[...] = jnp.exp(log_a)
    t = jnp.tanh(log_a)
    one_m = (-2.0 * t) / (1.0 - t)
    b_scr[...] = jnp.sqrt(one_m) * (ig * xc)


def _block_scan(a, b, reverse):
    row = lax.broadcasted_iota(jnp.int32, a.shape, 0)
    for d in (1, 2, 4):
        if reverse:
            keep = row < SUB - d
            shift = SUB - d
        else:
            keep = row >= d
            shift = d
        a_sh = jnp.where(keep, pltpu.roll(a, shift, axis=0), 1.0)
        b_sh = jnp.where(keep, pltpu.roll(b, shift, axis=0), 0.0)
        b = a * b_sh + b
        a = a * a_sh
    return a, b


def _rglru_scan(a_scr, b_scr, h_scr, emit, reverse):
    tc, w = a_scr.shape
    nblk = tc // SUB

    def body(k, h):
        kk = (nblk - 1 - k) if reverse else k
        rows = pl.ds(pl.multiple_of(kk * SUB, SUB), SUB)
        ap, bp = _block_scan(a_scr[rows, :], b_scr[rows, :], reverse)
        out = ap * h + bp
        emit(rows, out)
        edge = out[0:1, :] if reverse else out[SUB - 1:SUB, :]
        return jnp.broadcast_to(edge, (SUB, w))

    h_scr[...] = lax.fori_loop(0, nblk, body, h_scr[...], unroll=2)


def _rglru_fwd_kernel(x_ref, prev_ref, next_ref, cw_ref, cb_ref, wr_ref, br_ref, wi_ref, bi_ref, lam_ref,
                      hf_ref, a_scr, b_scr, r_scr, i_scr, h_scr, *, bpb):
    j = pl.program_id(1)

    @pl.when(j == 0)
    def _():
        h_scr[...] = jnp.zeros_like(h_scr)

    first = jnp.logical_or(j == 0, j == 1)
    last = jnp.logical_or(j == 0, j == bpb)
    _rglru_coeffs(x_ref, prev_ref, next_ref, cw_ref, cb_ref, wr_ref, br_ref, wi_ref, bi_ref, lam_ref,
                  a_scr, b_scr, r_scr, i_scr, first, last)

    def emit(rows, out):
        hf_ref[rows, :] = out

    _rglru_scan(a_scr, b_scr, h_scr, emit, reverse=False)


def _rglru_bwd_kernel(x_ref, prev_ref, next_ref, cw_ref, cb_ref, wr_ref, br_ref, wi_ref, bi_ref, lam_ref,
                      hf_ref, ga_ref, ya_ref, a_scr, b_scr, r_scr, i_scr, h_scr, *, bpb):
    j = pl.program_id(1)

    @pl.when(j == 0)
    def _():
        h_scr[...] = jnp.zeros_like(h_scr)

    first = jnp.logical_or(j == 0, j == bpb)
    last = jnp.logical_or(j == 0, j == 1)
    _rglru_coeffs(x_ref, prev_ref, next_ref, cw_ref, cb_ref, wr_ref, br_ref, wi_ref, bi_ref, lam_ref,
                  a_scr, b_scr, r_scr, i_scr, first, last)

    def emit(rows, out):
        ya_ref[rows, :] = ((hf_ref[rows, :] + out) * ga_ref[rows, :].astype(F32)).astype(ya_ref.dtype)

    _rglru_scan(a_scr, b_scr, h_scr, emit, reverse=True)


def _rglru(xa, ga, p, l, nbatch, bpb):
    rows, w = xa.shape
    hd = w // H_A
    lat_blocks = nbatch * bpb
    ga_blocks = ga.shape[0] // SEG
    nhalo = rows // 16
    per_blk = SEG // 16

    def common_specs(direction, blk):
        def prev_map(b, j):
            return (jnp.maximum(blk(b, j) * per_blk - 1, 0), 0)

        def next_map(b, j):
            return (jnp.minimum((blk(b, j) + 1) * per_blk, nhalo - 1), 0)

        vec = pl.BlockSpec((None, None, 1, w), lambda b, j: (l, direction, 0, 0))
        gw = pl.BlockSpec((None, None, H_A, hd, hd), lambda b, j: (l, direction, 0, 0, 0))
        return [pl.BlockSpec((SEG, w), lambda b, j: (blk(b, j), 0)),
                pl.BlockSpec((16, w), prev_map),
                pl.BlockSpec((16, w), next_map),
                pl.BlockSpec((None, CONV_A, w), lambda b, j: (l, 0, 0)),
                pl.BlockSpec((None, 1, w), lambda b, j: (l, 0, 0)),
                gw, vec, gw, vec, vec]

    def blk_f(b, j):
        return jnp.where(j == 0, lat_blocks + b, b * bpb + j - 1)

    def blk_b(b, j):
        return jnp.where(j == 0, lat_blocks + b, b * bpb + bpb - j)

    scratch = [pltpu.VMEM((SEG, w), F32)] * 4 + [pltpu.VMEM((SUB, w), F32)]
    weights = (p["rg_conv_w"], p["rg_conv_b"], p["rg_w_r"], p["rg_b_r"], p["rg_w_i"], p["rg_b_i"], p["rg_lambda"])
    hf = pl.pallas_call(
        functools.partial(_rglru_fwd_kernel, bpb=bpb),
        grid=(nbatch, bpb + 1),
        in_specs=common_specs(0, blk_f),
        out_specs=pl.BlockSpec((SEG, w), lambda b, j: (blk_f(b, j), 0)),
        out_shape=jax.ShapeDtypeStruct((rows, w), F32),
        scratch_shapes=scratch,
        compiler_params=_cp("arbitrary", "arbitrary"),
        name="rglru_forward",
    )(xa, xa, xa, *weights)
    return pl.pallas_call(
        functools.partial(_rglru_bwd_kernel, bpb=bpb),
        grid=(nbatch, bpb + 1),
        in_specs=common_specs(1, blk_b) + [pl.BlockSpec((SEG, w), lambda b, j: (blk_b(b, j), 0)),
                                           pl.BlockSpec((SEG, w), lambda b, j: (jnp.minimum(blk_b(b, j), ga_blocks - 1), 0))],
        out_specs=pl.BlockSpec((SEG, w), lambda b, j: (blk_b(b, j), 0)),
        out_shape=jax.ShapeDtypeStruct((rows, w), BF16),
        scratch_shapes=scratch,
        compiler_params=_cp("arbitrary", "arbitrary"),
        name="rglru_backward",
    )(xa, xa, xa, *weights, hf, ga)


def _layernorm(x, g, b):
    mu = jnp.mean(x, axis=-1, keepdims=True)
    xc = x - mu
    var = jnp.mean(xc * xc, axis=-1, keepdims=True)
    return xc * lax.rsqrt(var + EPS) * g + b


def _sgu_kernel(u_ref, v_ref, g_ref, b_ref, ws_ref, bs_ref, o_ref):
    tc, w = u_ref.shape
    gw = w // G_B
    v = _layernorm(v_ref[...].astype(F32), g_ref[...], b_ref[...]).astype(BF16)
    for n in range(tc // CHUNK):
        rs = slice(n * CHUNK, (n + 1) * CHUNK)
        for g in range(G_B):
            cs = slice(g * gw, (g + 1) * gw)
            s = jnp.dot(ws_ref[g].astype(BF16), v[rs, cs], preferred_element_type=F32)
            o_ref[rs, cs] = (u_ref[rs, cs].astype(F32) * (s + bs_ref[:, cs])).astype(o_ref.dtype)


def _sgu(uv, p, l):
    rows, w2 = uv.shape
    w = w2 // 2
    depth = p["sg_b"].shape[0]
    bs = jnp.repeat(jnp.swapaxes(p["sg_b"], 1, 2), w // G_B, axis=2)
    vec = pl.BlockSpec((None, 1, w), lambda i: (l, 0, 0))
    return pl.pallas_call(
        _sgu_kernel,
        grid=(rows // SEG,),
        in_specs=[pl.BlockSpec((SEG, w), lambda i: (i, 0)),
                  pl.BlockSpec((SEG, w), lambda i: (i, 1)),
                  vec, vec,
                  pl.BlockSpec((None, G_B, CHUNK, CHUNK), lambda i: (l, 0, 0, 0)),
                  pl.BlockSpec((None, CHUNK, w), lambda i: (l, 0, 0))],
        out_specs=pl.BlockSpec((SEG, w), lambda i: (i, 0)),
        out_shape=jax.ShapeDtypeStruct((rows, w), BF16),
        compiler_params=_cp("arbitrary"),
        name="spatial_gating",
    )(uv, uv, p["sg_ln_g"].reshape(depth, 1, w), p["sg_ln_b"].reshape(depth, 1, w), p["sg_w"], bs)


def _conv_lat_kernel(x_ref, cw_ref, cb_ref, g_ref, b_ref, o_ref, xp, ys):
    nr, wc, c = x_ref.shape
    half = CONV_C // 2
    zeros = jnp.zeros((half, wc, c), F32)
    xp[0:half, :, 0:c] = zeros
    xp[half + nr:half + nr + half, :, 0:c] = zeros
    xp[half:half + nr, :, 0:c] = x_ref[...].astype(F32)

    for g in range(c // LANES):
        ls = slice(g * LANES, (g + 1) * LANES)
        wv = [jnp.broadcast_to(cw_ref[k:k + 1, ls], (SUB, LANES)) for k in range(CONV_C)]

        def taps(r, carry, ls=ls, wv=wv):
            for hs in range(wc // SUB):
                ss = slice(hs * SUB, (hs + 1) * SUB)
                ys[r, ss, ls] = _tree_sum([wv[k] * xp[r + k, ss, ls] for k in range(CONV_C)])
            return carry

        lax.fori_loop(0, nr, taps, 0, unroll=2)

    def finish(r, carry):
        y = _layernorm(ys[r] + cb_ref[...], g_ref[...], b_ref[...])
        o_ref[r] = _silu(y).astype(o_ref.dtype)
        return carry

    lax.fori_loop(0, nr, finish, 0, unroll=4)


def _conv_ctx_kernel(x_ref, cw_ref, cb_ref, g_ref, b_ref, o_ref, xp):
    n, c = x_ref.shape
    half = CONV_C // 2
    pad = 16
    xp[0:pad, :] = jnp.zeros((pad, c), F32)
    xp[pad + n:pad + n + pad, :] = jnp.zeros((pad, c), F32)
    xp[pad:pad + n, :] = x_ref[...].astype(F32)
    acc = cw_ref[0:1, :] * xp[pad - half:pad - half + n, :]
    for k in range(1, CONV_C):
        off = pad - half + k
        acc = acc + cw_ref[k:k + 1, :] * xp[off:off + n, :]
    y = _layernorm(acc + cb_ref[...], g_ref[...], b_ref[...])
    o_ref[...] = _silu(y).astype(o_ref.dtype)


def _conformer(glu, p, l, nbatch, seq, ctx_len, with_ctx):
    rows, c = glu.shape
    depth = p["cv_b"].shape[0]
    nrow = seq // GRID_W
    wt = 16
    cw = p["cv_w"]
    vecs = [p["cv_b"].reshape(depth, 1, c), p["cv_ln_g"].reshape(depth, 1, c), p["cv_ln_b"].reshape(depth, 1, c)]
    g3 = glu.reshape(rows // GRID_W, GRID_W, c)
    half = CONV_C // 2

    lat = pl.pallas_call(
        _conv_lat_kernel,
        grid=(nbatch, GRID_W // wt),
        in_specs=[pl.BlockSpec((nrow, wt, c), lambda b, j: (b, j, 0)),
                  pl.BlockSpec((None, CONV_C, c), lambda b, j: (l, 0, 0))]
                 + [pl.BlockSpec((None, 1, c), lambda b, j: (l, 0, 0))] * 3,
        out_specs=pl.BlockSpec((nrow, wt, c), lambda b, j: (b, j, 0)),
        out_shape=jax.ShapeDtypeStruct((nbatch * nrow, GRID_W, c), BF16),
        scratch_shapes=[pltpu.VMEM((nrow + 2 * half, wt, c + LANES), F32), pltpu.VMEM((nrow, wt, c), F32)],
        compiler_params=_cp("arbitrary", "arbitrary"),
        name="conformer_conv_latent",
    )(g3, cw, *vecs)
    lat = lat.reshape(nbatch * seq, c)
    if not with_ctx:
        return lat
    cblk0 = nbatch * seq // ctx_len
    cx = pl.pallas_call(
        _conv_ctx_kernel,
        grid=(nbatch,),
        in_specs=[pl.BlockSpec((ctx_len, c), lambda b: (cblk0 + b, 0)),
                  pl.BlockSpec((None, CONV_C, c), lambda b: (l, 0, 0))]
                 + [pl.BlockSpec((None, 1, c), lambda b: (l, 0, 0))] * 3,
        out_specs=pl.BlockSpec((ctx_len, c), lambda b: (b, 0)),
        out_shape=jax.ShapeDtypeStruct((nbatch * ctx_len, c), BF16),
        scratch_shapes=[pltpu.VMEM((ctx_len + 32, c), F32)],
        compiler_params=_cp("arbitrary"),
        name="conformer_conv_context",
    )(glu, cw, *vecs)
    return jnp.concatenate([lat, cx], axis=0)


def _ffn_pre_kernel(x_ref, g_ref, sh_ref, sc_ref, wr_ref, hx_ref, aff_ref):
    d = x_ref.shape[1]
    dh = d // 2
    x = x_ref[...]
    y = x * lax.rsqrt(jnp.mean(x * x, axis=-1, keepdims=True) + EPS) * g_ref[...]
    h = y * (1.0 + sc_ref[...]) + sh_ref[...]
    w = wr_ref[...]
    h_hi = h.astype(BF16)
    h_lo = (h - h_hi.astype(F32)).astype(BF16)
    w_hi = w.astype(BF16)
    w_lo = (w - w_hi.astype(F32)).astype(BF16)
    both = jnp.dot(h_hi, jnp.concatenate([w_hi, w_lo], axis=1), preferred_element_type=F32)
    logits = both[:, :LANES] + (jnp.dot(h_lo, w_hi, preferred_element_type=F32) + both[:, LANES:])
    lane = lax.broadcasted_iota(jnp.int32, logits.shape, 1)
    valid = lane < N_EXP
    m = jnp.max(jnp.where(valid, logits, -jnp.inf), axis=-1, keepdims=True)
    ex = jnp.where(valid, jnp.exp(logits - m), 0.0)
    aff = ex / jnp.sum(ex, axis=-1, keepdims=True)
    aff_ref[...] = aff
    lo = pltpu.bitcast(h[:, :dh].astype(BF16).astype(F32), jnp.int32)
    hi = pltpu.bitcast(h[:, dh:].astype(BF16).astype(F32), jnp.int32)
    hx_ref[:, :dh] = jnp.bitwise_or(jnp.bitwise_and(hi, jnp.int32(-65536)),
                                    lax.shift_right_logical(lo, jnp.int32(16)))
    hx_ref[:, dh:] = pltpu.bitcast(aff, jnp.int32)


def _ffn_pre(x, rows, g, modt, wr_pad, l, lat_blocks, bpb):
    d = x.shape[1]
    blk = pl.BlockSpec((SEG, d), lambda i: (i, 0))

    def mod_spec(k):
        return pl.BlockSpec((None, None, 1, d),
                            lambda i: (l, _seg_of_block(i, lat_blocks, bpb) * N_MOD + k, 0, 0))

    return pl.pallas_call(
        _ffn_pre_kernel,
        grid=(rows // SEG,),
        in_specs=[blk,
                  pl.BlockSpec((None, 1, d), lambda i: (l, 0, 0)),
                  mod_spec(3), mod_spec(4),
                  pl.BlockSpec((None, d, LANES), lambda i: (l, 0, 0))],
        out_specs=[pl.BlockSpec((SEG, d // 2 + LANES), lambda i: (i, 0)),
                   pl.BlockSpec((SEG, LANES), lambda i: (i, 0))],
        out_shape=[jax.ShapeDtypeStruct((rows, d // 2 + LANES), jnp.int32),
                   jax.ShapeDtypeStruct((rows, LANES), F32)],
        compiler_params=_cp("arbitrary"),
        name="ffn_pre_router",
    )(x, g, modt, modt, wr_pad)


def _prefix_mats():
    r = lax.broadcasted_iota(jnp.int32, (LANES, LANES), 0)
    c = lax.broadcasted_iota(jnp.int32, (LANES, LANES), 1)
    incl = jnp.where(r <= c, 1.0, 0.0).astype(BF16)
    strict = jnp.where(c < r, 1.0, 0.0).astype(BF16)
    return incl, strict


def _topk_kernel(a_ref, o_ref, thr_scr, *, cap):
    nexp, nc, _ = a_ref.shape
    bits_all = pltpu.bitcast(a_ref[...], jnp.int32)
    thr_all = jnp.zeros((nexp, 1, 1), jnp.int32)
    for bit in range(30, -1, -1):
        cand = thr_all | jnp.int32(1 << bit)
        ge = jnp.where(bits_all >= cand, 1.0, 0.0)
        cnt_all = jnp.sum(jnp.sum(ge, axis=2, keepdims=True), axis=1, keepdims=True)
        thr_all = jnp.where(cnt_all >= float(cap), cand, thr_all)
    thr_scr[...] = jnp.broadcast_to(thr_all, thr_scr.shape)
    incl, strict = _prefix_mats()
    ones_sq = jnp.ones((LANES, LANES), BF16)
    ones8 = jnp.ones((SUB, LANES), BF16)
    lane_c = lax.broadcasted_iota(jnp.int32, (cap, LANES), 1)
    s_col = lax.broadcasted_iota(jnp.int32, (cap, LANES), 0).astype(F32)
    capf = jnp.float32(cap)
    o_ref[...] = jnp.zeros_like(o_ref)

    def total(x):
        return jnp.sum(jnp.sum(x, axis=1, keepdims=True), axis=0, keepdims=True)

    def pad_rows(x):
        if nc == LANES:
            return x
        return jnp.concatenate([x, jnp.zeros((LANES - nc, LANES), x.dtype)], axis=0)

    def prefix(mask_f):
        mp = pad_rows(mask_f).astype(BF16)
        within = jnp.dot(mp, incl, preferred_element_type=F32)
        tot = jnp.dot(mp, ones_sq, preferred_element_type=F32).astype(BF16)
        excl = jnp.dot(strict, tot, preferred_element_type=F32)
        return mp, within, within + excl

    def body(e, carry):
        bits = pltpu.bitcast(a_ref[e], jnp.int32)
        thr = thr_scr[e][0:1, :]
        gt = bits > thr
        eq = bits == thr
        need = capf - total(jnp.where(gt, 1.0, 0.0))
        _, _, pe = prefix(jnp.where(eq, 1.0, 0.0))
        take = jnp.logical_or(gt, jnp.logical_and(eq, pe[:nc] <= need))
        mp, within, _ = prefix(jnp.where(take, 1.0, 0.0))
        s_row = lax.dot_general(ones8, mp, (((1,), (1,)), ((), ())), preferred_element_type=F32)
        pend_row = jnp.dot(s_row.astype(BF16), incl, preferred_element_type=F32)
        pend_b = jnp.broadcast_to(pend_row[0:1, :], (cap, LANES))
        s_b = jnp.broadcast_to(s_row[0:1, :], (cap, LANES))
        before = jnp.logical_and(pend_b <= s_col, lane_c < nc)
        c_s = jnp.sum(jnp.where(before, 1.0, 0.0), axis=1, keepdims=True)
        p_excl = jnp.sum(jnp.where(before, s_b, 0.0), axis=1, keepdims=True)
        sel = jnp.where(lane_c.astype(F32) == c_s, 1.0, 0.0).astype(BF16)
        w_row = jnp.dot(sel, within.astype(BF16), preferred_element_type=F32)
        cnt = jnp.sum(jnp.where(w_row <= s_col - p_excl, 1.0, 0.0), axis=1, keepdims=True)
        idx = (c_s * float(LANES) + cnt).astype(jnp.int32)
        o_ref[...] = jnp.where(lane_c == e, idx, o_ref[...])
        return carry

    lax.fori_loop(0, nexp, body, 0)


def _topk(aff_t, cap):
    nsets, nexp, nc, _ = aff_t.shape
    return pl.pallas_call(
        functools.partial(_topk_kernel, cap=cap),
        grid=(nsets,),
        in_specs=[pl.BlockSpec((None, nexp, nc, LANES), lambda s: (s, 0, 0, 0))],
        out_specs=pl.BlockSpec((None, cap, LANES), lambda s: (s, 0, 0)),
        out_shape=jax.ShapeDtypeStruct((nsets, cap, LANES), jnp.int32),
        scratch_shapes=[pltpu.VMEM((nexp, SUB, LANES), jnp.int32)],
        compiler_params=_cp("arbitrary"),
        name="expert_choice_topk",
    )(aff_t)


def _expert_kernel(idx_ref, hx_hbm, wg_ref, wu_ref, wd_ref, *refs, nf, segs):
    nseg = len(segs)
    gm_refs = refs[:nseg]
    o_ref, xg, xlo, xhi, hid, gate, sem = refs[nseg:]
    e = pl.program_id(0)
    s = pl.program_id(1)
    nr = xg.shape[0] * SUB
    dh = xlo.shape[1]
    tf = wg_ref.shape[1]

    def row_copy(g, u, row):
        return pltpu.make_async_copy(hx_hbm.at[pl.ds(row, 1)], xg.at[g, pl.ds(u, 1)], sem)

    @pl.when(s == 0)
    def _():
        def issue(g, c):
            for u in range(SUB):
                row_copy(g, u, idx_ref[e * nr + g * SUB + u]).start(priority=u % 2)
            return c

        lax.fori_loop(0, nr // SUB, issue, 0)

        def drain(g, c):
            for u in range(SUB):
                row_copy(g, u, 0).wait()
            return c

        lax.fori_loop(0, nr // SUB, drain, 0)

        def unpack(k, c):
            rows = pl.ds(pl.multiple_of(k * ROW_CHUNK, ROW_CHUNK), ROW_CHUNK)
            gchunk = ROW_CHUNK // SUB
            packed = xg[pl.ds(k * gchunk, gchunk)].reshape(ROW_CHUNK, xg.shape[2])
            u = packed[:, :dh]
            xlo[rows, :] = pltpu.bitcast(lax.shift_left(u, jnp.int32(16)), F32).astype(BF16)
            xhi[rows, :] = pltpu.bitcast(jnp.bitwise_and(u, jnp.int32(-65536)), F32).astype(BF16)
            aff = pltpu.bitcast(packed[:, dh:], F32)
            lane = lax.broadcasted_iota(jnp.int32, aff.shape, 1)
            g = jnp.sum(jnp.where(lane == e, aff, 0.0), axis=1, keepdims=True)
            gate[rows, :] = jnp.broadcast_to(g, aff.shape)
            return c

        lax.fori_loop(0, nr // ROW_CHUNK, unpack, 0, unroll=2)

    @pl.when(s < nf)
    def _():
        def proj(w_ref):
            return (jnp.dot(xlo[...], w_ref[:dh, :].astype(BF16), preferred_element_type=F32)
                    + jnp.dot(xhi[...], w_ref[dh:, :].astype(BF16), preferred_element_type=F32))

        hid[s] = (_silu(proj(wg_ref)) * proj(wu_ref)).astype(BF16)

    @pl.when(s >= nf)
    def _():
        acc = jnp.dot(hid[0], wd_ref[0:tf, :].astype(BF16), preferred_element_type=F32)
        for f in range(1, nf):
            acc = acc + jnp.dot(hid[f], wd_ref[f * tf:(f + 1) * tf, :].astype(BF16),
                                preferred_element_type=F32)
        for (r0, r1), gm_ref in zip(segs, gm_refs):
            for k in range(o_ref.shape[1] // LANES):
                cs = slice(k * LANES, (k + 1) * LANES)
                o_ref[r0:r1, cs] = (acc[r0:r1, cs] * gate[r0:r1, :] * gm_ref[:, cs]).astype(o_ref.dtype)


def _experts(idx_flat, hx, w_g, w_u, w_d, modt, l, nr, segs, seg_ids):
    d, dexp = w_g.shape[-2:]
    tf = 256 if dexp % 256 == 0 else dexp
    td = 1024 if d % 1024 == 0 else d
    nf = dexp // tf
    nd = d // td
    wcols = hx.shape[1]

    def gm_spec(seg):
        return pl.BlockSpec((None, None, 1, td),
                            lambda e, s, idx: (l, seg * N_MOD + 5, 0, jnp.maximum(s - nf, 0)))

    grid_spec = pltpu.PrefetchScalarGridSpec(
        num_scalar_prefetch=1,
        grid=(N_EXP, nf + nd),
        in_specs=[pl.BlockSpec(memory_space=pl.ANY),
                  pl.BlockSpec((None, None, d, tf), lambda e, s, idx: (l, e, 0, jnp.minimum(s, nf - 1))),
                  pl.BlockSpec((None, None, d, tf), lambda e, s, idx: (l, e, 0, jnp.minimum(s, nf - 1))),
                  pl.BlockSpec((None, None, dexp, td), lambda e, s, idx: (l, e, 0, jnp.maximum(s - nf, 0)))]
                 + [gm_spec(sg) for sg in seg_ids],
        out_specs=pl.BlockSpec((None, nr, td), lambda e, s, idx: (e, 0, jnp.maximum(s - nf, 0))),
        scratch_shapes=[pltpu.VMEM((nr // SUB, SUB, wcols), jnp.int32),
                        pltpu.VMEM((nr, d // 2), BF16),
                        pltpu.VMEM((nr, d // 2), BF16),
                        pltpu.VMEM((nf, nr, tf), BF16),
                        pltpu.VMEM((nr, LANES), F32),
                        pltpu.SemaphoreType.DMA(())],
    )
    return pl.pallas_call(
        functools.partial(_expert_kernel, nf=nf, segs=segs),
        grid_spec=grid_spec,
        out_shape=jax.ShapeDtypeStruct((N_EXP, nr, d), BF16),
        compiler_params=_cp("arbitrary", "arbitrary"),
        name="experts_swiglu",
    )(idx_flat, hx, w_g, w_u, w_d, *([modt] * len(seg_ids)))


def _scatter_kernel(idx_ref, ye_ref, x_in, x_out, buf, rsem, wsem):
    del x_in
    e = pl.program_id(0)
    ngroups, _, d = buf.shape
    nr = ngroups * SUB
    gchunk = ROW_CHUNK // SUB

    def rd(g, u, row, h):
        return pltpu.make_async_copy(x_out.at[pl.ds(row, 1)], buf.at[g, pl.ds(u, 1)], rsem.at[h])

    def wr(g, u, row, h):
        return pltpu.make_async_copy(buf.at[g, pl.ds(u, 1)], x_out.at[pl.ds(row, 1)], wsem.at[h])

    def each(h, fn):
        def body(g, carry):
            for u in range(SUB):
                fn(g, u)
            return carry

        lax.fori_loop(h * (ngroups // 2), (h + 1) * (ngroups // 2), body, 0)

    def add_half(h):
        def add(k, carry):
            rows = pl.ds(pl.multiple_of(k * ROW_CHUNK, ROW_CHUNK), ROW_CHUNK)
            groups = pl.ds(k * gchunk, gchunk)
            buf[groups] = buf[groups] + ye_ref[rows, :].astype(F32).reshape(gchunk, SUB, d)
            return carry

        lax.fori_loop(h * (nr // ROW_CHUNK // 2), (h + 1) * (nr // ROW_CHUNK // 2), add, 0)

    for h in range(2):
        each(h, lambda g, u, h=h: rd(g, u, idx_ref[e * nr + g * SUB + u], h).start(priority=u % 2))
    for h in range(2):
        each(h, lambda g, u, h=h: rd(g, u, 0, h).wait())
        add_half(h)
        each(h, lambda g, u, h=h: wr(g, u, idx_ref[e * nr + g * SUB + u], h).start(priority=u % 2))
    for h in range(2):
        each(h, lambda g, u, h=h: wr(g, u, 0, h).wait())


def _scatter_add(idx_flat, ye, x):
    nexp, nr, d = ye.shape
    grid_spec = pltpu.PrefetchScalarGridSpec(
        num_scalar_prefetch=1,
        grid=(nexp,),
        in_specs=[pl.BlockSpec((None, nr, d), lambda e, idx: (e, 0, 0)),
                  pl.BlockSpec(memory_space=pl.ANY)],
        out_specs=pl.BlockSpec(memory_space=pl.ANY),
        scratch_shapes=[pltpu.VMEM((nr // SUB, SUB, d), F32),
                        pltpu.SemaphoreType.DMA((2,)), pltpu.SemaphoreType.DMA((2,))],
    )
    assert nr % (2 * ROW_CHUNK) == 0
    return pl.pallas_call(
        _scatter_kernel,
        grid_spec=grid_spec,
        out_shape=jax.ShapeDtypeStruct(x.shape, x.dtype),
        input_output_aliases={2: 0},
        compiler_params=_cp("arbitrary"),
        name="scatter_add_residual",
    )(idx_flat, ye, x)


def _final_norm_kernel(x_ref, g_ref, o_ref):
    x = x_ref[...]
    o_ref[...] = x * lax.rsqrt(jnp.mean(x * x, axis=-1, keepdims=True) + EPS) * g_ref[...]


def _final_norm(x, g, nrows):
    d = x.shape[1]
    return pl.pallas_call(
        _final_norm_kernel,
        grid=(nrows // SEG,),
        in_specs=[pl.BlockSpec((SEG, d), lambda i: (i, 0)), pl.BlockSpec((1, d), lambda i: (0, 0))],
        out_specs=pl.BlockSpec((SEG, d), lambda i: (i, 0)),
        out_shape=jax.ShapeDtypeStruct((nrows, d), F32),
        compiler_params=_cp("arbitrary"),
        name="final_norm",
    )(x, g.reshape(1, d))


def _route(aff, nbatch, seq, ctx_len, with_ctx):
    lat = nbatch * seq
    cap_l = CAP_FACTOR * seq // N_EXP
    cap_c = CAP_FACTOR * ctx_len // N_EXP
    a_l = aff[:lat, :N_EXP].reshape(nbatch, seq, N_EXP).transpose(0, 2, 1)
    a_l = a_l.reshape(nbatch, N_EXP, seq // LANES, LANES)
    idx_l = _topk(a_l, cap_l)[:, :, :N_EXP]
    idx_l = idx_l + (jnp.arange(nbatch, dtype=jnp.int32) * seq)[:, None, None]
    segs = tuple((b * cap_l, (b + 1) * cap_l) for b in range(nbatch))
    seg_ids = tuple(range(nbatch))
    if not with_ctx:
        idx = idx_l.transpose(2, 0, 1).reshape(N_EXP, nbatch * cap_l)
        return idx.reshape(-1), idx.shape[1], segs, seg_ids
    cpad = SUB * LANES
    a_c = aff[lat:, :N_EXP].reshape(nbatch, ctx_len, N_EXP).transpose(0, 2, 1)
    a_c = jnp.pad(a_c, ((0, 0), (0, 0), (0, cpad - ctx_len)), constant_values=-1.0)
    a_c = a_c.reshape(nbatch, N_EXP, SUB, LANES)
    idx_c = _topk(a_c, cap_c)[:, :, :N_EXP]
    idx_c = idx_c + (lat + jnp.arange(nbatch, dtype=jnp.int32) * ctx_len)[:, None, None]
    idx = jnp.concatenate([idx_l.transpose(2, 0, 1).reshape(N_EXP, nbatch * cap_l),
                           idx_c.transpose(2, 0, 1).reshape(N_EXP, nbatch * cap_c)], axis=1)
    segs = segs + ((nbatch * cap_l, nbatch * (cap_l + cap_c)),)
    seg_ids = seg_ids + (nbatch,)
    return idx.reshape(-1), idx.shape[1], segs, seg_ids


def kernel(x, c, ctx, c_ctx, w_mod, b_mod, g_mix, g_ffn, w_in, b_in, rg_conv_w, rg_conv_b, rg_w_r, rg_b_r, rg_w_i, rg_b_i, rg_lambda, sg_ln_g, sg_ln_b, sg_w, sg_b, cv_w, cv_b, cv_ln_g, cv_ln_b, w_proj_a, w_proj_b, w_proj_c, w_out, w_router, w_e_gate, w_e_up, w_e_down, g_final):
    nbatch, seq, d = x.shape
    ctx_len = ctx.shape[1]
    depth = w_mod.shape[0]
    w_a = rg_conv_w.shape[-1]
    w_b = sg_ln_g.shape[-1]
    w_c = cv_w.shape[-1]
    assert ctx_len == SEG and seq % SEG == 0 and nbatch + 1 <= SUB
    bpb = seq // SEG
    lat_blocks = nbatch * bpb
    lat_rows = nbatch * seq
    all_rows = lat_rows + nbatch * ctx_len
    off_xa = w_a
    off_uv = 2 * w_a
    off_glu = off_uv + 2 * w_b
    off_gate = off_glu + 2 * w_c

    cvec = jnp.concatenate([c, c_ctx[None], jnp.zeros((SUB - nbatch - 1, d), F32)], axis=0)
    modt = _mod_all(cvec, w_mod, b_mod).reshape(depth, SUB * N_MOD, 1, d)

    b_in3 = b_in.reshape(depth, 1, -1)
    gmix3 = g_mix.reshape(depth, 1, d)
    gffn3 = g_ffn.reshape(depth, 1, d)
    wr_pad = jnp.pad(w_router, ((0, 0), (0, 0), (0, LANES - N_EXP)))
    rg = {"rg_conv_w": rg_conv_w, "rg_conv_b": rg_conv_b.reshape(depth, 1, w_a),
          "rg_w_r": rg_w_r, "rg_b_r": rg_b_r.reshape(depth, 2, 1, w_a),
          "rg_w_i": rg_w_i, "rg_b_i": rg_b_i.reshape(depth, 2, 1, w_a),
          "rg_lambda": rg_lambda.reshape(depth, 2, 1, w_a)}
    sg = {"sg_ln_g": sg_ln_g, "sg_ln_b": sg_ln_b, "sg_w": sg_w, "sg_b": sg_b}
    cv = {"cv_w": cv_w, "cv_b": cv_b, "cv_ln_g": cv_ln_g, "cv_ln_b": cv_ln_b}

    for l in range(depth):
        need_ctx = l < depth - 1
        nrows = all_rows if need_ctx else lat_rows
        if l == 0:
            xall, h = _first_norm_mod(x.reshape(lat_rows, d), ctx.reshape(all_rows - lat_rows, d),
                                      gmix3, modt, l, lat_blocks, bpb)
        else:
            h = _norm_mod(xall, gmix3, modt, l, 0, 1, lat_blocks, bpb)
        ga =_mm(h, w_in, b_in3, l, 0, w_a, _gelu, BF16, nrows)
        xa = _mm(h, w_in, b_in3, l, off_xa, w_a, lambda z: z, BF16, all_rows)
        uv = _mm(h, w_in, b_in3, l, off_uv, 2 * w_b, _gelu, BF16, nrows)
        glu = _mm_glu(h, w_in, b_in3, l, off_glu, w_c, nrows)
        gates = _mm(h, w_in, b_in3, l, off_gate, N_BRANCH * d, _sigmoid, BF16, nrows)
        ya = _rglru(xa, ga, rg, l, nbatch, bpb)
        yb = _sgu(uv, sg, l)
        yc = _conformer(glu, cv, l, nbatch, seq, ctx_len, need_ctx)
        y = _merge(ya, yb, yc, w_proj_a, w_proj_b, w_proj_c, gates, l)
        xall = _mm_out_residual(y, w_out, xall, modt, l, seq, lat_rows)
        hx, aff = _ffn_pre(xall, nrows, gffn3, modt, wr_pad, l, lat_blocks, bpb)
        idx_flat, nr, segs, seg_ids = _route(aff, nbatch, seq, ctx_len, need_ctx)
        ye = _experts(idx_flat, hx, w_e_gate, w_e_up, w_e_down, modt, l, nr, segs, seg_ids)
        xall = _scatter_add(idx_flat, ye, xall)
    return _final_norm(xall, g_final, lat_rows).reshape(nbatch, seq, d)
```
